```python
import math
import jax, jax.numpy as jnp
from jax import lax
import numpy as np

D_MODEL = 1024
BATCH = 4
SEQ = 8192
DEPTH = 2

N_AB_LAYERS = (DEPTH + 1) // 2
N_NSA_LAYERS = DEPTH // 2
DEEPNORM_ALPHA = (2.0 * DEPTH) ** 0.25
DEEPNORM_BETA = (8.0 * DEPTH) ** -0.25
LN_EPS = 1e-5

POOL_WIDTH = D_MODEL // 2
POOL_WINDOWS = (2, 4, 8, 16)
POOL_GROUP = POOL_WIDTH // len(POOL_WINDOWS)
CONV_CH = D_MODEL // 2
CONV_TAPS = 31
AB_IN = POOL_WIDTH + 2 * CONV_CH
AB_CAT = POOL_WIDTH + CONV_CH

N_HEADS = 16
HEAD_DIM = 64
N_KV_GROUPS = 4
HEADS_PER_GROUP = N_HEADS // N_KV_GROUPS
KV_WIDTH = N_KV_GROUPS * HEAD_DIM
CMP_BLOCK = 32
CMP_STRIDE = 16
CMP_HIDDEN = 128
SLC_BLOCK = 64
N_SELECT = 16
WINDOW = 512
Q_BLOCK = 128
N_BRANCH = 3
ROPE_THETA = 500000.0
ROPE_DIM = HEAD_DIM // 4
SLC_PER_CMP = SLC_BLOCK // CMP_STRIDE
SLC_OVERLAP_W = (1.0, 2.0, 2.0, 2.0, 1.0)
NSA_IN = N_HEADS * HEAD_DIM + 6 * KV_WIDTH + N_HEADS * N_BRANCH

N_EXPERTS = 256
TOP_K = 8
N_EXPERT_GROUPS = 8
TOPK_GROUPS = 4
EXPERT_FF = 256
SHARED_FF = 256
ROUTED_SCALE = 2.5
EXPERT_ROW_BLOCK = 128

kernel_name = "hybrid_pool_conv_nsa_moe_deepnorm"


def layer_norm(x, g, b):
    xf = x.astype(jnp.float32)
    mu = xf.mean(-1, keepdims=True)
    var = jnp.square(xf - mu).mean(-1, keepdims=True)
    y = (xf - mu) * lax.rsqrt(var + LN_EPS)
    return (y * g.astype(jnp.float32) + b.astype(jnp.float32)).astype(x.dtype)


def masked_softmax(s, mask):
    s = jnp.where(mask, s.astype(jnp.float32), -jnp.inf)
    m = jnp.max(s, -1, keepdims=True)
    m = jnp.where(jnp.isfinite(m), m, 0.0)
    e = jnp.exp(s - m)
    d = e.sum(-1, keepdims=True)
    return e / jnp.where(d > 0, d, 1.0)


def partial_rotary(x, pos):
    half = ROPE_DIM // 2
    inv_freq = jnp.power(jnp.float32(ROPE_THETA), -jnp.arange(0, ROPE_DIM, 2, dtype=jnp.float32) / ROPE_DIM)
    ang = pos.astype(jnp.float32)[:, None] * inv_freq[None, :]
    cos = jnp.cos(ang)[:, None, :]
    sin = jnp.sin(ang)[:, None, :]
    xf = x.astype(jnp.float32)
    x1, x2 = xf[..., :half], xf[..., half:ROPE_DIM]
    rot = jnp.concatenate([x1 * cos - x2 * sin, x2 * cos + x1 * sin], -1)
    return jnp.concatenate([rot.astype(x.dtype), x[..., ROPE_DIM:]], -1)


def causal_pool_minus_self(u, window):
    T = u.shape[1]
    uf = u.astype(jnp.float32)
    c = jnp.cumsum(uf, axis=1)
    prev = jnp.pad(c, ((0, 0), (window, 0), (0, 0)))[:, :T]
    count = jnp.minimum(jnp.arange(1, T + 1), window).astype(jnp.float32)
    return ((c - prev) / count[None, :, None] - uf).astype(u.dtype)


def pool_conv_mixer(x, w_in, pool_mix, pool_scale, conv_w, conv_b, conv_ln_g, conv_ln_b, w_out):
    B, T, _ = x.shape
    h = x @ w_in
    u = h[..., :POOL_WIDTH]
    val = h[..., POOL_WIDTH:POOL_WIDTH + CONV_CH]
    gate = h[..., POOL_WIDTH + CONV_CH:]
    p = jnp.stack([causal_pool_minus_self(u[..., gi * POOL_GROUP:(gi + 1) * POOL_GROUP], w)
                   for gi, w in enumerate(POOL_WINDOWS)], axis=2)
    a = jnp.einsum('btgc,gcd->btgd', p, pool_mix).reshape(B, T, POOL_WIDTH) * pool_scale
    v = val * jax.nn.sigmoid(gate)
    c = lax.conv_general_dilated(v, conv_w, window_strides=(1,), padding=((CONV_TAPS - 1, 0),),
                                 dimension_numbers=('NWC', 'WIO', 'NWC'),
                                 feature_group_count=CONV_CH) + conv_b
    c = jax.nn.silu(layer_norm(c, conv_ln_g, conv_ln_b))
    return jnp.concatenate([a, c], -1) @ w_out


def compress_blocks(kv, pos_emb, w1, w2):
    B, T, G, dh = kv.shape
    chunks = kv.reshape(B, T // CMP_STRIDE, CMP_STRIDE, G, dh)
    blocks = jnp.concatenate([chunks[:, :-1], chunks[:, 1:]], axis=2)
    blocks = blocks + pos_emb[None, None, :, None, :]
    nc = blocks.shape[1]
    flat = jnp.moveaxis(blocks, 3, 2).reshape(B, nc, G, CMP_BLOCK * dh)
    return jax.nn.gelu(flat @ w1) @ w2


def nsa_sequence(q, q_rot, kc, vc, ks, vs, kw, vw, gates):
    T = q.shape[0]
    nc = kc.shape[0]
    ns = T // SLC_BLOCK
    n_sel = min(N_SELECT, ns)
    n_qb = T // Q_BLOCK
    G, HPG, dh = N_KV_GROUPS, HEADS_PER_GROUP, HEAD_DIM
    scale = HEAD_DIM ** -0.5
    cmp_end = jnp.arange(nc) * CMP_STRIDE + CMP_BLOCK - 1
    blk_ids = jnp.arange(ns)
    ks_blocks = jnp.moveaxis(ks.reshape(ns, SLC_BLOCK, G, dh), 2, 0)
    vs_blocks = jnp.moveaxis(vs.reshape(ns, SLC_BLOCK, G, dh), 2, 0)
    kw_pad = jnp.pad(kw, ((WINDOW, 0), (0, 0), (0, 0)))
    vw_pad = jnp.pad(vw, ((WINDOW, 0), (0, 0), (0, 0)))
    right_pad = SLC_PER_CMP * ns + 1 - (nc + 1)
    grp = jnp.arange(G)[:, None, None]

    def block(qi):
        s0 = qi * Q_BLOCK
        t = s0 + jnp.arange(Q_BLOCK)
        qb = lax.dynamic_slice_in_dim(q, s0, Q_BLOCK, 0).reshape(Q_BLOCK, G, HPG, dh)
        qr = lax.dynamic_slice_in_dim(q_rot, s0, Q_BLOCK, 0).reshape(Q_BLOCK, G, HPG, dh)
        gb = lax.dynamic_slice_in_dim(gates, s0, Q_BLOCK, 0).reshape(Q_BLOCK, G, HPG, N_BRANCH)
        sc = jnp.einsum('qghd,ngd->ghqn', qb, kc) * scale
        pc = masked_softmax(sc, (cmp_end[None, :] <= t[:, None])[None, None])
        o_cmp = jnp.einsum('ghqn,ngd->qghd', pc, vc)
        imp_c = jnp.pad(pc.sum(1), ((0, 0), (0, 0), (1, right_pad)))
        imp = sum(SLC_OVERLAP_W[k] * imp_c[..., k:k + SLC_PER_CMP * ns:SLC_PER_CMP]
                  for k in range(len(SLC_OVERLAP_W)))
        cur = t // SLC_BLOCK
        forced = (blk_ids[None, :] == 0) | (blk_ids[None, :] == cur[:, None]) | (blk_ids[None, :] == cur[:, None] - 1)
        valid = blk_ids[None, :] * SLC_BLOCK <= t[:, None]
        imp = jnp.where(forced[None], jnp.inf, jnp.where(valid[None], imp, -jnp.inf))
        _, sel = lax.top_k(imp, n_sel)
        k_sel = ks_blocks[grp, sel]
        v_sel = vs_blocks[grp, sel].reshape(G, Q_BLOCK, n_sel * SLC_BLOCK, dh)
        tok = sel[..., None] * SLC_BLOCK + jnp.arange(SLC_BLOCK)
        m_sel = (tok <= t[None, :, None, None]).reshape(G, Q_BLOCK, n_sel * SLC_BLOCK)[:, None]
        ss = jnp.einsum('qghd,gqskd->ghqsk', qr, k_sel).reshape(G, HPG, Q_BLOCK, n_sel * SLC_BLOCK) * scale
        ps = masked_softmax(ss, m_sel)
        o_slc = jnp.einsum('ghqk,gqkd->qghd', ps, v_sel)
        kwb = lax.dynamic_slice_in_dim(kw_pad, s0, WINDOW + Q_BLOCK, 0)
        vwb = lax.dynamic_slice_in_dim(vw_pad, s0, WINDOW + Q_BLOCK, 0)
        kpos = s0 - WINDOW + jnp.arange(WINDOW + Q_BLOCK)
        dlt = t[:, None] - kpos[None, :]
        m_win = (dlt >= 0) & (dlt < WINDOW) & (kpos[None, :] >= 0)
        sw = jnp.einsum('qghd,kgd->ghqk', qr, kwb) * scale
        pw = masked_softmax(sw, m_win[None, None])
        o_win = jnp.einsum('ghqk,kgd->qghd', pw, vwb)
        o = gb[..., 0:1] * o_cmp + gb[..., 1:2] * o_slc + gb[..., 2:3] * o_win
        return o.reshape(Q_BLOCK, N_HEADS * HEAD_DIM).astype(q.dtype)

    return lax.map(block, jnp.arange(n_qb)).reshape(T, N_HEADS * HEAD_DIM)


def nsa_mixer(x, w_in, cmp_pos_k, cmp_w1_k, cmp_w2_k, cmp_pos_v, cmp_w1_v, cmp_w2_v, w_out):
    B, T, _ = x.shape
    h = x @ w_in
    splits = tuple(int(s) for s in np.cumsum([N_HEADS * HEAD_DIM] + [KV_WIDTH] * 6))
    q, kc, vc, ks, vs, kw, vw, g = jnp.split(h, splits, axis=-1)
    q = q.reshape(B, T, N_HEADS, HEAD_DIM)
    kv_shape = (B, T, N_KV_GROUPS, HEAD_DIM)
    kc, vc, ks, vs, kw, vw = (a.reshape(kv_shape) for a in (kc, vc, ks, vs, kw, vw))
    gates = jax.nn.sigmoid(g.astype(jnp.float32)).reshape(B, T, N_HEADS, N_BRANCH)
    pos = jnp.arange(T)
    q_rot = partial_rotary(q, pos)
    ks = partial_rotary(ks, pos)
    kw = partial_rotary(kw, pos)
    kc = compress_blocks(kc, cmp_pos_k, cmp_w1_k, cmp_w2_k)
    vc = compress_blocks(vc, cmp_pos_v, cmp_w1_v, cmp_w2_v)
    o = lax.map(lambda a: nsa_sequence(*a), (q, q_rot, kc, vc, ks, vs, kw, vw, gates))
    return o @ w_out


def moe_ffn(x, w_router, b_router, w_gate, w_up, w_down, ws_gate, ws_up, ws_down):
    B, T, D = x.shape
    N = B * T
    h = x.reshape(N, D)
    scores = jax.nn.sigmoid((h @ w_router).astype(jnp.float32))
    biased = scores + b_router.astype(jnp.float32)
    per_group = N_EXPERTS // N_EXPERT_GROUPS
    grp_score = lax.top_k(biased.reshape(N, N_EXPERT_GROUPS, per_group), 2)[0].sum(-1)
    _, top_grp = lax.top_k(grp_score, TOPK_GROUPS)
    grp_mask = (top_grp[..., None] == jnp.arange(N_EXPERT_GROUPS)).any(1)
    exp_mask = jnp.repeat(grp_mask, per_group, axis=1)
    _, idx = lax.top_k(jnp.where(exp_mask, biased, -jnp.inf), TOP_K)
    wts = jnp.take_along_axis(scores, idx, axis=1)
    wts = wts / wts.sum(-1, keepdims=True) * ROUTED_SCALE
    A = N * TOP_K
    M = EXPERT_ROW_BLOCK
    flat_e = idx.reshape(A)
    flat_tok = jnp.arange(A, dtype=jnp.int32) // TOP_K
    flat_w = wts.reshape(A)
    order = jnp.argsort(flat_e)
    sorted_e = flat_e[order]
    counts = jnp.bincount(flat_e, length=N_EXPERTS)
    padded = (counts + M - 1) // M * M
    start = jnp.cumsum(counts) - counts
    pend = jnp.cumsum(padded)
    pstart = pend - padded
    dest = pstart[sorted_e] + jnp.arange(A) - start[sorted_e]
    n_blocks = -(-(A + N_EXPERTS * (M - 1)) // M)
    P = n_blocks * M
    row_tok = jnp.zeros((P,), jnp.int32).at[dest].set(flat_tok[order])
    row_w = jnp.zeros((P,), jnp.float32).at[dest].set(flat_w[order])
    block_exp = jnp.minimum(jnp.searchsorted(pend, jnp.arange(n_blocks) * M, side='right'), N_EXPERTS - 1)

    def step(acc, blk):
        toks, rw, e = blk
        xb = h[toks]
        hid = jax.nn.silu(xb @ w_gate[e]) * (xb @ w_up[e])
        yb = (hid @ w_down[e]).astype(jnp.float32) * rw[:, None]
        return acc.at[toks].add(yb), None

    routed, _ = lax.scan(step, jnp.zeros((N, D), jnp.float32),
                         (row_tok.reshape(n_blocks, M), row_w.reshape(n_blocks, M), block_exp))
    shared = (jax.nn.silu(h @ ws_gate) * (h @ ws_up)) @ ws_down
    return (routed.astype(h.dtype) + shared).reshape(B, T, D)


def setup_inputs(seed: int = 0) -> dict:
    key = jax.random.key(seed)
    keys = iter(jax.random.split(key, 40))
    def nrm(shape, scale):
        return jax.random.normal(next(keys), shape, jnp.float32) * scale
    D = D_MODEL
    NA, NN = N_AB_LAYERS, N_NSA_LAYERS
    beta = DEEPNORM_BETA
    return {
        "x": nrm((BATCH, SEQ, D), 1.0),
        "ab_w_in": nrm((NA, D, AB_IN), D ** -0.5),
        "ab_pool_mix": nrm((NA, len(POOL_WINDOWS), POOL_GROUP, POOL_GROUP), POOL_GROUP ** -0.5),
        "ab_pool_scale": 1.0 + nrm((NA, POOL_WIDTH), 0.1),
        "ab_conv_w": nrm((NA, CONV_TAPS, 1, CONV_CH), CONV_TAPS ** -0.5),
        "ab_conv_b": nrm((NA, CONV_CH), 0.02),
        "ab_conv_ln_g": 1.0 + nrm((NA, CONV_CH), 0.05),
        "ab_conv_ln_b": nrm((NA, CONV_CH), 0.02),
        "ab_w_out": nrm((NA, AB_CAT, D), beta * AB_CAT ** -0.5),
        "nsa_w_in": nrm((NN, D, NSA_IN), D ** -0.5),
        "nsa_cmp_pos_k": nrm((NN, CMP_BLOCK, HEAD_DIM), 0.1),
        "nsa_cmp_w1_k": nrm((NN, CMP_BLOCK * HEAD_DIM, CMP_HIDDEN), (CMP_BLOCK * HEAD_DIM) ** -0.5),
        "nsa_cmp_w2_k": nrm((NN, CMP_HIDDEN, HEAD_DIM), CMP_HIDDEN ** -0.5),
        "nsa_cmp_pos_v": nrm((NN, CMP_BLOCK, HEAD_DIM), 0.1),
        "nsa_cmp_w1_v": nrm((NN, CMP_BLOCK * HEAD_DIM, CMP_HIDDEN), (CMP_BLOCK * HEAD_DIM) ** -0.5),
        "nsa_cmp_w2_v": nrm((NN, CMP_HIDDEN, HEAD_DIM), CMP_HIDDEN ** -0.5),
        "nsa_w_out": nrm((NN, N_HEADS * HEAD_DIM, D), beta * (N_HEADS * HEAD_DIM) ** -0.5),
        "ln_mix_g": 1.0 + nrm((DEPTH, D), 0.05),
        "ln_mix_b": nrm((DEPTH, D), 0.02),
        "ln_ffn_g": 1.0 + nrm((DEPTH, D), 0.05),
        "ln_ffn_b": nrm((DEPTH, D), 0.02),
        "moe_w_router": nrm((DEPTH, D, N_EXPERTS), D ** -0.5),
        "moe_b_router": nrm((DEPTH, N_EXPERTS), 0.01),
        "moe_w_gate": nrm((DEPTH, N_EXPERTS, D, EXPERT_FF), D ** -0.5),
        "moe_w_up": nrm((DEPTH, N_EXPERTS, D, EXPERT_FF), D ** -0.5),
        "moe_w_down": nrm((DEPTH, N_EXPERTS, EXPERT_FF, D), beta * EXPERT_FF ** -0.5),
        "moe_ws_gate": nrm((DEPTH, D, SHARED_FF), D ** -0.5),
        "moe_ws_up": nrm((DEPTH, D, SHARED_FF), D ** -0.5),
        "moe_ws_down": nrm((DEPTH, SHARED_FF, D), beta * SHARED_FF ** -0.5),
    }


def reference(x, ab_w_in, ab_pool_mix, ab_pool_scale, ab_conv_w, ab_conv_b, ab_conv_ln_g, ab_conv_ln_b,
              ab_w_out, nsa_w_in, nsa_cmp_pos_k, nsa_cmp_w1_k, nsa_cmp_w2_k, nsa_cmp_pos_v, nsa_cmp_w1_v,
              nsa_cmp_w2_v, nsa_w_out, ln_mix_g, ln_mix_b, ln_ffn_g, ln_ffn_b, moe_w_router, moe_b_router,
              moe_w_gate, moe_w_up, moe_w_down, moe_ws_gate, moe_ws_up, moe_ws_down):
    for i in range(DEPTH):
        j = i // 2
        if i % 2 == 0:
            mix = pool_conv_mixer(x, ab_w_in[j], ab_pool_mix[j], ab_pool_scale[j], ab_conv_w[j], ab_conv_b[j],
                                  ab_conv_ln_g[j], ab_conv_ln_b[j], ab_w_out[j])
        else:
            mix = nsa_mixer(x, nsa_w_in[j], nsa_cmp_pos_k[j], nsa_cmp_w1_k[j], nsa_cmp_w2_k[j],
                            nsa_cmp_pos_v[j], nsa_cmp_w1_v[j], nsa_cmp_w2_v[j], nsa_w_out[j])
        x = layer_norm(DEEPNORM_ALPHA * x + mix, ln_mix_g[i], ln_mix_b[i])
        ffn = moe_ffn(x, moe_w_router[i], moe_b_router[i], moe_w_gate[i], moe_w_up[i], moe_w_down[i],
                      moe_ws_gate[i], moe_ws_up[i], moe_ws_down[i])
        x = layer_norm(DEEPNORM_ALPHA * x + ffn, ln_ffn_g[i], ln_ffn_b[i])
    return x
```

```python
import functools

import jax
import jax.numpy as jnp
import numpy as np
from jax import lax
from jax.experimental import pallas as pl
from jax.experimental.pallas import tpu as pltpu

F32 = jnp.float32
BF16 = jnp.bfloat16
I32 = jnp.int32

D_MODEL = 1024
DEPTH = 2
DEEPNORM_ALPHA = (2.0 * DEPTH) ** 0.25
LN_EPS = 1e-5

POOL_WIDTH = 512
POOL_WINDOWS = (2, 4, 8, 16)
POOL_GROUP = 128
CONV_CH = 512
CONV_TAPS = 31
AB_IN = POOL_WIDTH + 2 * CONV_CH
HALO = 32
CONV_ROWS = 32

N_HEADS = 16
HEAD_DIM = 64
N_KV_GROUPS = 4
HEADS_PER_GROUP = 4
KV_WIDTH = 256
CMP_BLOCK = 32
CMP_STRIDE = 16
CMP_HIDDEN = 128
SLC_BLOCK = 64
N_SELECT = 16
WINDOW = 512
N_BRANCH = 3
ROPE_THETA = 500000.0
ROPE_DIM = 16
SLC_OVERLAP_W = (1.0, 2.0, 2.0, 2.0, 1.0)

N_EXPERTS = 256
TOP_K = 8
N_EXPERT_GROUPS = 8
TOPK_GROUPS = 4
EXPERT_FF = 256
SHARED_FF = 256
ROUTED_SCALE = 2.5
EXPERT_ROWS = 128

VMEM_LIMIT = 56 * 1024 * 1024
NEG_INF = float("-inf")


def _cparams(*sem):
    return pltpu.CompilerParams(dimension_semantics=sem, vmem_limit_bytes=VMEM_LIMIT)


def _layer_norm(y, g, b):
    mu = jnp.mean(y, axis=-1, keepdims=True)
    d = y - mu
    var = jnp.mean(d * d, axis=-1, keepdims=True)
    return d * lax.rsqrt(var + LN_EPS) * g + b


def _silu(v):
    return v * jax.nn.sigmoid(v)


def _ab_mixer_kernel(x_ref, win_ref, pmix_ref, pscale_ref, cw_ref, cb_ref, clg_ref, clb_ref,
                     wout_ref, lng_ref, lnb_ref, o_ref, eu_ref, ev_ref, cat_ref, *, tq):
    ti = pl.program_id(1)

    @pl.when(ti == 0)
    def _():
        eu_ref[0:HALO, :] = jnp.zeros((HALO, POOL_WIDTH), F32)
        ev_ref[0:HALO, :] = jnp.zeros((HALO, CONV_CH), F32)

    @pl.when(ti > 0)
    def _():
        eu_ref[0:HALO, :] = eu_ref[tq:tq + HALO, :]
        ev_ref[0:HALO, :] = ev_ref[tq:tq + HALO, :]

    x = x_ref[...]
    h = jnp.dot(x.astype(BF16), win_ref[...], preferred_element_type=F32)
    u = h[:, :POOL_WIDTH]
    eu_ref[HALO:HALO + tq, :] = u
    ev_ref[HALO:HALO + tq, :] = h[:, POOL_WIDTH:POOL_WIDTH + CONV_CH] * jax.nn.sigmoid(
        h[:, POOL_WIDTH + CONV_CH:])

    pos1 = ti * tq + lax.broadcasted_iota(I32, (tq, 1), 0) + 1
    for gi, w in enumerate(POOL_WINDOWS):
        lo, hi = gi * POOL_GROUP, (gi + 1) * POOL_GROUP
        ug = u[:, lo:hi]
        s = ug
        for j in range(1, w):
            s = s + eu_ref[HALO - j:HALO - j + tq, lo:hi]
        cnt = jnp.minimum(pos1, w).astype(F32)
        p = s / cnt - ug
        a = jnp.dot(p.astype(BF16), pmix_ref[gi], preferred_element_type=F32)
        cat_ref[:, lo:hi] = (a * pscale_ref[:, lo:hi]).astype(BF16)

    for r in range(tq // CONV_ROWS):
        base = r * CONV_ROWS + HALO - (CONV_TAPS - 1)
        acc = jnp.broadcast_to(cb_ref[...], (CONV_ROWS, CONV_CH))
        for k in range(CONV_TAPS):
            acc = acc + cw_ref[k:k + 1, :] * ev_ref[base + k:base + k + CONV_ROWS, :]
        c = _silu(_layer_norm(acc, clg_ref[...], clb_ref[...]))
        cat_ref[r * CONV_ROWS:(r + 1) * CONV_ROWS, POOL_WIDTH:] = c.astype(BF16)

    mix = jnp.dot(cat_ref[...], wout_ref[...], preferred_element_type=F32)
    o_ref[...] = _layer_norm(DEEPNORM_ALPHA * x + mix, lng_ref[...], lnb_ref[...])


def ab_mixer(x, w_in, pool_mix, pool_scale, conv_w, conv_b, conv_ln_g, conv_ln_b, w_out, ln_g, ln_b,
             tq=256):
    B, T, D = x.shape
    cw = jnp.pad(conv_w.reshape(CONV_TAPS, CONV_CH), ((0, 32 - CONV_TAPS), (0, 0)))
    row = lambda v: v.reshape(1, -1).astype(F32)
    const = lambda shape: pl.BlockSpec(shape, lambda b, t: (0,) * len(shape))
    return pl.pallas_call(
        functools.partial(_ab_mixer_kernel, tq=tq),
        grid=(B, T // tq),
        in_specs=[
            pl.BlockSpec((None, tq, D), lambda b, t: (b, t, 0)),
            const((D, AB_IN)), const((4, POOL_GROUP, POOL_GROUP)), const((1, POOL_WIDTH)),
            const((32, CONV_CH)), const((1, CONV_CH)), const((1, CONV_CH)), const((1, CONV_CH)),
            const((D, D)), const((1, D)), const((1, D)),
        ],
        out_specs=pl.BlockSpec((None, tq, D), lambda b, t: (b, t, 0)),
        out_shape=jax.ShapeDtypeStruct((B, T, D), F32),
        scratch_shapes=[pltpu.VMEM((HALO + tq, POOL_WIDTH), F32),
                        pltpu.VMEM((HALO + tq, CONV_CH), F32),
                        pltpu.VMEM((tq, D), BF16)],
        compiler_params=_cparams("arbitrary", "arbitrary"),
        name="ab_mixer",
    )(x, w_in.astype(BF16), pool_mix.astype(BF16), row(pool_scale), cw, row(conv_b),
      row(conv_ln_g), row(conv_ln_b), w_out.astype(BF16), row(ln_g), row(ln_b))


def _first_index(hit_src, m, iota, size):
    return jnp.min(jnp.where(hit_src == m, iota, size), axis=0, keepdims=True)


def _router_kernel(x_ref, wh_ref, wl_ref, b_ref, tri_ref, idx_ref, wt_ref, rank_ref, cnt_ref, *, tq):
    @pl.when(pl.program_id(0) == 0)
    def _():
        cnt_ref[...] = jnp.zeros_like(cnt_ref)

    x = x_ref[...]
    xh = x.astype(BF16)
    xl = (x - xh.astype(F32)).astype(BF16)
    nt = (((1,), (1,)), ((), ()))
    wh, wl = wh_ref[...], wl_ref[...]
    logits = (lax.dot_general(wh, xh, nt, preferred_element_type=F32)
              + lax.dot_general(wh, xl, nt, preferred_element_type=F32)
              + lax.dot_general(wl, xh, nt, preferred_element_type=F32))
    scores = jax.nn.sigmoid(logits)
    biased = scores + b_ref[...]

    per_group = N_EXPERTS // N_EXPERT_GROUPS
    io_g = lax.broadcasted_iota(I32, (per_group, tq), 0)
    gs_rows = []
    for g in range(N_EXPERT_GROUPS):
        sg = biased[g * per_group:(g + 1) * per_group, :]
        m1 = jnp.max(sg, axis=0, keepdims=True)
        i1 = _first_index(sg, m1, io_g, per_group)
        m2 = jnp.max(jnp.where(io_g == i1, NEG_INF, sg), axis=0, keepdims=True)
        gs_rows.append(m1 + m2)
    gs = jnp.concatenate(gs_rows, axis=0)

    io_ng = lax.broadcasted_iota(I32, (N_EXPERT_GROUPS, tq), 0)
    grp_sel = jnp.zeros((N_EXPERT_GROUPS, tq), jnp.bool_)
    for _ in range(TOPK_GROUPS):
        m = jnp.max(gs, axis=0, keepdims=True)
        hit = io_ng == _first_index(gs, m, io_ng, N_EXPERT_GROUPS)
        grp_sel = grp_sel | hit
        gs = jnp.where(hit, NEG_INF, gs)
    exp_mask = jnp.concatenate(
        [jnp.broadcast_to(grp_sel[g:g + 1, :], (per_group, tq)) for g in range(N_EXPERT_GROUPS)], axis=0)

    masked = jnp.where(exp_mask, biased, NEG_INF)
    io_e = lax.broadcasted_iota(I32, (N_EXPERTS, tq), 0)
    idx_rows, w_rows = [], []
    onehot = jnp.zeros((N_EXPERTS, tq), F32)
    for _ in range(TOP_K):
        m = jnp.max(masked, axis=0, keepdims=True)
        first = _first_index(masked, m, io_e, N_EXPERTS)
        hit = io_e == first
        idx_rows.append(first)
        w_rows.append(jnp.sum(jnp.where(hit, scores, 0.0), axis=0, keepdims=True))
        onehot = onehot + hit.astype(F32)
        masked = jnp.where(hit, NEG_INF, masked)
    w = jnp.concatenate(w_rows, axis=0)
    wt_ref[...] = w / jnp.sum(w, axis=0, keepdims=True) * ROUTED_SCALE
    idx_ref[...] = jnp.concatenate(idx_rows, axis=0)

    before = jnp.dot(onehot.astype(BF16), tri_ref[...], preferred_element_type=F32) + cnt_ref[...]
    rank_ref[...] = jnp.concatenate(
        [jnp.sum(jnp.where(io_e == r, before, 0.0), axis=0, keepdims=True) for r in idx_rows],
        axis=0).astype(I32)
    cnt_ref[...] += jnp.sum(onehot, axis=1, keepdims=True)


def moe_router(x, w_router, b_router, tq=256):
    N, D = x.shape
    wt = w_router.T
    wh = wt.astype(BF16)
    wl = (wt - wh.astype(F32)).astype(BF16)
    tri = (np.arange(tq)[:, None] < np.arange(tq)[None, :]).astype(np.float32)
    const = lambda shape: pl.BlockSpec(shape, lambda i: (0,) * len(shape))
    tok = lambda rows: pl.BlockSpec((rows, tq), lambda i: (0, i))
    return pl.pallas_call(
        functools.partial(_router_kernel, tq=tq),
        grid=(N // tq,),
        in_specs=[pl.BlockSpec((tq, D), lambda i: (i, 0)), const((N_EXPERTS, D)), const((N_EXPERTS, D)),
                  const((N_EXPERTS, 1)), const((tq, tq))],
        out_specs=[tok(TOP_K), tok(TOP_K), tok(TOP_K), const((N_EXPERTS, 1))],
        out_shape=[jax.ShapeDtypeStruct((TOP_K, N), I32), jax.ShapeDtypeStruct((TOP_K, N), F32),
                   jax.ShapeDtypeStruct((TOP_K, N), I32), jax.ShapeDtypeStruct((N_EXPERTS, 1), F32)],
        compiler_params=_cparams("arbitrary"),
        name="moe_router",
    )(x, wh, wl, b_router.reshape(N_EXPERTS, 1).astype(F32), jnp.asarray(tri, BF16))


def _row_copy(src_hbm, row, dst, slot, sem):
    return pltpu.make_async_copy(src_hbm.at[pl.ds(row, 1), :], dst.at[pl.ds(slot, 1), :], sem)


def _expert_kernel(bexp_ref, nused_ref, tok_ref, x_hbm, wg_ref, wu_ref, wd_ref, y_ref, xbuf, sem, *, m):
    b = pl.program_id(0)

    @pl.when(b < nused_ref[0])
    def _():
        def issue(r, c):
            _row_copy(x_hbm, tok_ref[0, 0, r], xbuf, r, sem).start()
            return c
        lax.fori_loop(0, m, issue, 0)

        def wait(r, c):
            _row_copy(x_hbm, 0, xbuf, r, sem).wait()
            return c
        lax.fori_loop(0, m, wait, 0)

        xb = xbuf[...].astype(BF16)
        g = jnp.dot(xb, wg_ref[0].astype(BF16), preferred_element_type=F32)
        u = jnp.dot(xb, wu_ref[0].astype(BF16), preferred_element_type=F32)
        hid = (_silu(g) * u).astype(BF16)
        y_ref[...] = jnp.dot(hid, wd_ref[0].astype(BF16), preferred_element_type=F32)

    @pl.when(b >= nused_ref[0])
    def _():
        y_ref[...] = jnp.zeros_like(y_ref)


def moe_experts(x, row_tok, block_exp, n_used, w_gate, w_up, w_down):
    N, D = x.shape
    m = EXPERT_ROWS
    n_blocks = row_tok.shape[0] // m
    grid_spec = pltpu.PrefetchScalarGridSpec(
        num_scalar_prefetch=2,
        grid=(n_blocks,),
        in_specs=[
            pl.BlockSpec((1, 1, m), lambda b, be, nu: (b, 0, 0), memory_space=pltpu.SMEM),
            pl.BlockSpec(memory_space=pl.ANY),
            pl.BlockSpec((1, D, EXPERT_FF), lambda b, be, nu: (be[b], 0, 0)),
            pl.BlockSpec((1, D, EXPERT_FF), lambda b, be, nu: (be[b], 0, 0)),
            pl.BlockSpec((1, EXPERT_FF, D), lambda b, be, nu: (be[b], 0, 0)),
        ],
        out_specs=pl.BlockSpec((m, D), lambda b, be, nu: (b, 0)),
        scratch_shapes=[pltpu.VMEM((m, D), F32), pltpu.SemaphoreType.DMA(())],
    )
    return pl.pallas_call(
        functools.partial(_expert_kernel, m=m),
        grid_spec=grid_spec,
        out_shape=jax.ShapeDtypeStruct((n_blocks * m, D), F32),
        compiler_params=_cparams("arbitrary"),
        name="moe_experts",
    )(block_exp, n_used, row_tok.reshape(n_blocks, 1, m), x, w_gate, w_up, w_down)


def _combine_kernel(dest_ref, y_hbm, w_ref, x_ref, wsg_ref, wsu_ref, wsd_ref, lng_ref, lnb_ref, o_ref,
                    ybuf, sem, *, tq):
    for k in range(TOP_K):
        def issue(r, c, k=k):
            _row_copy(y_hbm, dest_ref[0, k, r], ybuf.at[k], r, sem).start()
            return c
        lax.fori_loop(0, tq, issue, 0)

    x = x_ref[...]
    xb = x.astype(BF16)
    g = jnp.dot(xb, wsg_ref[...], preferred_element_type=F32)
    u = jnp.dot(xb, wsu_ref[...], preferred_element_type=F32)
    shared = jnp.dot((_silu(g) * u).astype(BF16), wsd_ref[...], preferred_element_type=F32)

    for k in range(TOP_K):
        def wait(r, c, k=k):
            _row_copy(y_hbm, 0, ybuf.at[k], r, sem).wait()
            return c
        lax.fori_loop(0, tq, wait, 0)

    w = w_ref[...]
    routed = w[:, 0:1] * ybuf[0]
    for k in range(1, TOP_K):
        routed = routed + w[:, k:k + 1] * ybuf[k]
    o_ref[...] = _layer_norm(DEEPNORM_ALPHA * x + (routed + shared), lng_ref[...], lnb_ref[...])


def moe_combine(x, y_sorted, dest, w_tok, ws_gate, ws_up, ws_down, ln_g, ln_b, tq=128):
    N, D = x.shape
    row = lambda v: v.reshape(1, -1).astype(F32)
    const = lambda shape: pl.BlockSpec(shape, lambda i: (0,) * len(shape))
    dest_blocks = dest.reshape(TOP_K, N // tq, tq).transpose(1, 0, 2)
    return pl.pallas_call(
        functools.partial(_combine_kernel, tq=tq),
        grid=(N // tq,),
        in_specs=[
            pl.BlockSpec((1, TOP_K, tq), lambda i: (i, 0, 0), memory_space=pltpu.SMEM),
            pl.BlockSpec(memory_space=pl.ANY),
            pl.BlockSpec((tq, TOP_K), lambda i: (i, 0)),
            pl.BlockSpec((tq, D), lambda i: (i, 0)),
            const((D, SHARED_FF)), const((D, SHARED_FF)), const((SHARED_FF, D)), const((1, D)), const((1, D)),
        ],
        out_specs=pl.BlockSpec((tq, D), lambda i: (i, 0)),
        out_shape=jax.ShapeDtypeStruct((N, D), F32),
        scratch_shapes=[pltpu.VMEM((TOP_K, tq, D), F32), pltpu.SemaphoreType.DMA(())],
        compiler_params=_cparams("arbitrary"),
        name="moe_combine",
    )(dest_blocks, y_sorted, w_tok, x, ws_gate.astype(BF16), ws_up.astype(BF16), ws_down.astype(BF16),
      row(ln_g), row(ln_b))


def moe_layer(x, w_router, b_router, w_gate, w_up, w_down, ws_gate, ws_up, ws_down, ln_g, ln_b):
    N, D = x.shape
    m = EXPERT_ROWS
    idx, wts, rank, counts = moe_router(x, w_router, b_router)
    counts = counts.reshape(N_EXPERTS).astype(I32)
    padded = (counts + m - 1) // m * m
    pend = jnp.cumsum(padded)
    pstart = pend - padded
    dest = pstart[idx] + rank
    n_blocks = -(-(N * TOP_K + N_EXPERTS * (m - 1)) // m)
    tok_ids = jnp.broadcast_to(jnp.arange(N, dtype=I32)[None, :], (TOP_K, N))
    row_tok = jnp.zeros((n_blocks * m,), I32).at[dest.reshape(-1)].set(tok_ids.reshape(-1))
    block_exp = jnp.minimum(
        jnp.searchsorted(pend, jnp.arange(n_blocks, dtype=I32) * m, side="right"), N_EXPERTS - 1).astype(I32)
    n_used = (pend[-1:] // m).astype(I32)
    y_sorted = moe_experts(x, row_tok, block_exp, n_used, w_gate, w_up, w_down)
    return moe_combine(x, y_sorted, dest, wts.T, ws_gate, ws_up, ws_down, ln_g, ln_b)


Q_TILE = 128
SEL_KEYS = 512
WIN_TILE = 128
WIN_KEYS = WINDOW + Q_TILE
PROJ_TQ = 512
NSA_TOK_COLS = N_HEADS * HEAD_DIM + 4 * KV_WIDTH
NSA_FEAT_ROWS = 2 * KV_WIDTH + N_HEADS * N_BRANCH
NT_DIMS = (((1,), (1,)), ((), ()))


def _rotary(v, cos, sin, lo_mask):
    parts = []
    for c in range(v.shape[1] // 128):
        blk = v[:, c * 128:(c + 1) * 128]
        nxt = pltpu.roll(blk, 128 - ROPE_DIM // 2, axis=1)
        prv = pltpu.roll(blk, ROPE_DIM // 2, axis=1)
        parts.append(blk * cos + jnp.where(lo_mask, nxt, prv) * sin)
    return jnp.concatenate(parts, axis=1)


def _nsa_proj_kernel(x_ref, wtok_ref, wfeat_ref, cos_ref, sin_ref, q_ref, qr_ref, kc_ref, vc_ref, ks_ref,
                     kw_ref, vs_ref, vw_ref, g_ref, *, tq):
    xb = x_ref[...].astype(BF16)
    h = jnp.dot(xb, wtok_ref[...], preferred_element_type=F32)
    cos, sin = cos_ref[...], sin_ref[...]
    lo_mask = (lax.broadcasted_iota(I32, (tq, 128), 1) % HEAD_DIM) < ROPE_DIM // 2
    nq = N_HEADS * HEAD_DIM
    q = h[:, :nq] * (HEAD_DIM ** -0.5)
    q_ref[...] = q.astype(BF16)
    qr_ref[...] = _rotary(q, cos, sin, lo_mask).astype(BF16)
    kc_ref[...] = h[:, nq:nq + KV_WIDTH].astype(BF16)
    vc_ref[...] = h[:, nq + KV_WIDTH:nq + 2 * KV_WIDTH].astype(BF16)
    ks_ref[...] = _rotary(h[:, nq + 2 * KV_WIDTH:nq + 3 * KV_WIDTH], cos, sin, lo_mask).astype(BF16)
    kw_ref[...] = _rotary(h[:, nq + 3 * KV_WIDTH:nq + 4 * KV_WIDTH], cos, sin, lo_mask).astype(BF16)
    hf = lax.dot_general(wfeat_ref[...], xb, NT_DIMS, preferred_element_type=F32)
    vs_ref[...] = hf[:KV_WIDTH].astype(BF16)
    for j in range(tq // WIN_TILE):
        vw_ref[j] = hf[KV_WIDTH:2 * KV_WIDTH, j * WIN_TILE:(j + 1) * WIN_TILE].astype(BF16)
    g_ref[...] = jax.nn.sigmoid(hf[2 * KV_WIDTH:])


def nsa_project(x, w_in, tq=PROJ_TQ):
    B, T, D = x.shape
    nq = N_HEADS * HEAD_DIM
    q, kc, vc, ks, vs, kw, vw, g = jnp.split(w_in, [nq + i * KV_WIDTH for i in range(7)], axis=1)
    wtok = jnp.concatenate([q, kc, vc, ks, kw], axis=1).astype(BF16)
    wfeat = jnp.concatenate([vs, vw, g], axis=1).T.astype(BF16)
    inv_freq = jnp.power(jnp.float32(ROPE_THETA), -jnp.arange(0, ROPE_DIM, 2, dtype=F32) / ROPE_DIM)
    ang = jnp.arange(T).astype(F32)[:, None] * inv_freq[None, :]
    ones = jnp.ones((T, HEAD_DIM - ROPE_DIM), F32)
    cos_h = jnp.concatenate([jnp.cos(ang), jnp.cos(ang), ones], axis=1)
    sin_h = jnp.concatenate([-jnp.sin(ang), jnp.sin(ang), 0.0 * ones], axis=1)
    cos_t = jnp.concatenate([cos_h, cos_h], axis=1)
    sin_t = jnp.concatenate([sin_h, sin_h], axis=1)
    nt = T // tq
    tokm = lambda w: pl.BlockSpec((None, tq, w), lambda b, t: (b, t, 0))
    const = lambda shape: pl.BlockSpec(shape, lambda b, t: (0,) * len(shape))
    outs = pl.pallas_call(
        functools.partial(_nsa_proj_kernel, tq=tq),
        grid=(B, nt),
        in_specs=[tokm(D), const((D, NSA_TOK_COLS)), const((NSA_FEAT_ROWS, D)),
                  pl.BlockSpec((tq, 128), lambda b, t: (t, 0)), pl.BlockSpec((tq, 128), lambda b, t: (t, 0))],
        out_specs=[tokm(nq), tokm(nq), tokm(KV_WIDTH), tokm(KV_WIDTH), tokm(KV_WIDTH), tokm(KV_WIDTH),
                   pl.BlockSpec((None, None, KV_WIDTH, tq), lambda b, t: (b, t, 0, 0)),
                   pl.BlockSpec((None, tq // WIN_TILE, KV_WIDTH, WIN_TILE), lambda b, t: (b, t, 0, 0)),
                   pl.BlockSpec((None, N_HEADS * N_BRANCH, tq), lambda b, t: (b, 0, t))],
        out_shape=[jax.ShapeDtypeStruct((B, T, nq), BF16), jax.ShapeDtypeStruct((B, T, nq), BF16),
                   jax.ShapeDtypeStruct((B, T, KV_WIDTH), BF16), jax.ShapeDtypeStruct((B, T, KV_WIDTH), BF16),
                   jax.ShapeDtypeStruct((B, T, KV_WIDTH), BF16), jax.ShapeDtypeStruct((B, T, KV_WIDTH), BF16),
                   jax.ShapeDtypeStruct((B, T // tq, KV_WIDTH, tq), BF16),
                   jax.ShapeDtypeStruct((B, T // WIN_TILE, KV_WIDTH, WIN_TILE), BF16),
                   jax.ShapeDtypeStruct((B, N_HEADS * N_BRANCH, T), F32)],
        compiler_params=_cparams("arbitrary", "arbitrary"),
        name="nsa_project",
    )(x, wtok, wfeat, cos_t, sin_t)
    return outs


def _compress_kernel(x_ref, pos_ref, w1_ref, w2_ref, o_ref):
    blocks = (x_ref[...].astype(F32) + pos_ref[...]).astype(BF16)
    pre = jnp.dot(blocks, w1_ref[...], preferred_element_type=F32)
    hid = jax.nn.gelu(pre, approximate=True)
    o_ref[...] = jnp.dot(hid.astype(BF16), w2_ref[...], preferred_element_type=F32).astype(BF16)


def nsa_compress(kv, pos_emb, w1, w2, tr=512):
    B, T, _ = kv.shape
    nch = T // CMP_STRIDE
    ch = kv.reshape(B, nch, CMP_STRIDE, N_KV_GROUPS, HEAD_DIM)
    blocks = jnp.concatenate([ch[:, :-1], ch[:, 1:]], axis=2)
    flat = jnp.moveaxis(blocks, 3, 2).reshape(B, nch - 1, N_KV_GROUPS, CMP_BLOCK * HEAD_DIM)
    flat = jnp.pad(flat, ((0, 0), (0, 1), (0, 0), (0, 0))).reshape(B * nch * N_KV_GROUPS, CMP_BLOCK * HEAD_DIM)
    rows = flat.shape[0]
    tr = min(tr, rows)
    const = lambda shape: pl.BlockSpec(shape, lambda i: (0,) * len(shape))
    out = pl.pallas_call(
        _compress_kernel,
        grid=(rows // tr,),
        in_specs=[pl.BlockSpec((tr, CMP_BLOCK * HEAD_DIM), lambda i: (i, 0)), const((1, CMP_BLOCK * HEAD_DIM)),
                  const((CMP_BLOCK * HEAD_DIM, CMP_HIDDEN)), const((CMP_HIDDEN, HEAD_DIM))],
        out_specs=pl.BlockSpec((tr, HEAD_DIM), lambda i: (i, 0)),
        out_shape=jax.ShapeDtypeStruct((rows, HEAD_DIM), BF16),
        compiler_params=_cparams("arbitrary"),
        name="nsa_compress",
    )(flat, pos_emb.reshape(1, -1).astype(F32), w1.astype(BF16), w2.astype(BF16))
    return out.reshape(B, nch, N_KV_GROUPS * HEAD_DIM)


def _softmax_keys(s, mask):
    s = jnp.where(mask, s, NEG_INF)
    m = jnp.max(s, axis=0, keepdims=True)
    m = jnp.where((m > NEG_INF) & (m < float("inf")), m, 0.0)
    e = jnp.exp(s - m)
    d = jnp.sum(e, axis=0, keepdims=True)
    return e / jnp.where(d > 0, d, 1.0)


def _nsa_attn_kernel(q_ref, qr_ref, kc_ref, vct_ref, ks_ref, vst_ref, kw_ref, vwt_ref, g_ref, mt_ref, o_ref,
                     sel_ref, *, qb):
    t0 = pl.program_id(1) * qb
    hpg = HEADS_PER_GROUP
    t_row = t0 + lax.broadcasted_iota(I32, (1, qb), 1)
    t_all = jnp.concatenate([t_row] * hpg, axis=1)
    q, qr = q_ref[...], qr_ref[...]
    gates = g_ref[...]
    n_cmp = kc_ref.shape[0]
    n_slc = mt_ref.shape[0]
    heads = []
    for g in range(N_KV_GROUPS):
        lanes = slice(g * HEAD_DIM, (g + 1) * HEAD_DIM)
        stack = lambda a: jnp.concatenate(
            [a[:, (g * hpg + hp) * HEAD_DIM:(g * hpg + hp + 1) * HEAD_DIM] for hp in range(hpg)], axis=0)
        qg, qrg = stack(q), stack(qr)

        sc = lax.dot_general(kc_ref[:, lanes], qg, NT_DIMS, preferred_element_type=F32)
        cmp_end = lax.broadcasted_iota(I32, (n_cmp, 1), 0) * CMP_STRIDE + (CMP_BLOCK - 1)
        pc = _softmax_keys(sc, cmp_end <= t_all)
        o_cmp = jnp.dot(vct_ref[lanes, :], pc.astype(BF16), preferred_element_type=F32)

        pcs = pc[:, :qb]
        for hp in range(1, hpg):
            pcs = pcs + pc[:, hp * qb:(hp + 1) * qb]
        hi = pcs.astype(BF16)
        r1 = pcs - hi.astype(F32)
        mid = r1.astype(BF16)
        low = (r1 - mid.astype(F32)).astype(BF16)
        mt = mt_ref[...]
        imp = (jnp.dot(mt, hi, preferred_element_type=F32) + jnp.dot(mt, mid, preferred_element_type=F32)
               + jnp.dot(mt, low, preferred_element_type=F32))
        j_io = lax.broadcasted_iota(I32, (n_slc, 1), 0)
        cur = t_row // SLC_BLOCK
        forced = (j_io == 0) | (j_io == cur) | (j_io == cur - 1)
        imp = jnp.where(forced, float("inf"), jnp.where(j_io * SLC_BLOCK <= t_row, imp, NEG_INF))
        sel = jnp.zeros((n_slc, qb), jnp.bool_)
        for _ in range(min(N_SELECT, n_slc)):
            m = jnp.max(imp, axis=0, keepdims=True)
            hit = j_io == jnp.min(jnp.where(imp == m, j_io, n_slc), axis=0, keepdims=True)
            sel = sel | hit
            imp = jnp.where(hit, NEG_INF, imp)
        sel_ref[...] = sel.astype(F32)

        def sel_tile(kt, carry):
            m_run, l_run, acc = carry
            k0 = pl.multiple_of(kt * SEL_KEYS, SEL_KEYS)
            s = lax.dot_general(ks_ref[pl.ds(k0, SEL_KEYS), lanes], qrg, NT_DIMS, preferred_element_type=F32)
            rows = lax.broadcasted_iota(I32, (SEL_KEYS, n_slc), 0)
            cols = lax.broadcasted_iota(I32, (SEL_KEYS, n_slc), 1)
            expand = (cols == kt * (SEL_KEYS // SLC_BLOCK) + rows // SLC_BLOCK).astype(BF16)
            picked = jnp.dot(expand, sel_ref[...].astype(BF16), preferred_element_type=F32)
            kpos = k0 + lax.broadcasted_iota(I32, (SEL_KEYS, 1), 0)
            mask1 = (picked > 0.5) & (kpos <= t_row)
            s = jnp.where(jnp.concatenate([mask1] * hpg, axis=1), s, NEG_INF)
            m_new = jnp.maximum(m_run, jnp.max(s, axis=0, keepdims=True))
            m_safe = jnp.where(m_new > NEG_INF, m_new, 0.0)
            p = jnp.exp(s - m_safe)
            alpha = jnp.exp(m_run - m_safe)
            l_new = alpha * l_run + jnp.sum(p, axis=0, keepdims=True)
            pv = jnp.dot(vst_ref[kt, lanes, :], p.astype(BF16), preferred_element_type=F32)
            return m_new, l_new, alpha * acc + pv

        init = (jnp.full((1, hpg * qb), NEG_INF, F32), jnp.zeros((1, hpg * qb), F32),
                jnp.zeros((HEAD_DIM, hpg * qb), F32))
        _, l_fin, acc = lax.fori_loop(0, t0 // SEL_KEYS + 1, sel_tile, init)
        o_sel = acc / jnp.where(l_fin > 0, l_fin, 1.0)

        start = pl.multiple_of(jnp.maximum(t0 - WINDOW, 0), WIN_TILE)
        sw = lax.dot_general(kw_ref[pl.ds(start, WIN_KEYS), lanes], qrg, NT_DIMS, preferred_element_type=F32)
        dlt = t_all - (start + lax.broadcasted_iota(I32, (WIN_KEYS, 1), 0))
        pw = _softmax_keys(sw, (dlt >= 0) & (dlt < WINDOW))
        w0 = start // WIN_TILE
        vwb = jnp.concatenate([vwt_ref[w0 + j, lanes, :] for j in range(WIN_KEYS // WIN_TILE)], axis=1)
        o_win = jnp.dot(vwb, pw.astype(BF16), preferred_element_type=F32)

        for hp in range(hpg):
            r = (g * hpg + hp) * N_BRANCH
            cols = slice(hp * qb, (hp + 1) * qb)
            heads.append(gates[r:r + 1] * o_cmp[:, cols] + gates[r + 1:r + 2] * o_sel[:, cols]
                         + gates[r + 2:r + 3] * o_win[:, cols])
    o_ref[...] = jnp.concatenate(heads, axis=0).T.astype(BF16)


def nsa_attention(q, qr, kc, vct, ks, vst, kw, vwt, gates, qb=Q_TILE):
    B, T, nq = q.shape
    n_cmp = kc.shape[1]
    n_slc = T // SLC_BLOCK
    mt = np.zeros((n_slc, n_cmp), np.float32)
    per = SLC_BLOCK // CMP_STRIDE
    for j in range(n_slc):
        for k, wk in enumerate(SLC_OVERLAP_W):
            n = per * j + k - 1
            if 0 <= n < n_cmp - 1:
                mt[j, n] = wk
    per_b = lambda shape: pl.BlockSpec((None,) + shape, lambda b, i: (b,) + (0,) * len(shape))
    return pl.pallas_call(
        functools.partial(_nsa_attn_kernel, qb=qb),
        grid=(B, T // qb),
        in_specs=[pl.BlockSpec((None, qb, nq), lambda b, i: (b, i, 0)),
                  pl.BlockSpec((None, qb, nq), lambda b, i: (b, i, 0)),
                  per_b((n_cmp, KV_WIDTH)), per_b((KV_WIDTH, n_cmp)),
                  per_b((T, KV_WIDTH)), per_b((T // SEL_KEYS, KV_WIDTH, SEL_KEYS)),
                  per_b((T, KV_WIDTH)), per_b((T // WIN_TILE, KV_WIDTH, WIN_TILE)),
                  pl.BlockSpec((None, N_HEADS * N_BRANCH, qb), lambda b, i: (b, 0, i)),
                  pl.BlockSpec((n_slc, n_cmp), lambda b, i: (0, 0))],
        out_specs=pl.BlockSpec((None, qb, nq), lambda b, i: (b, i, 0)),
        out_shape=jax.ShapeDtypeStruct((B, T, nq), BF16),
        scratch_shapes=[pltpu.VMEM((n_slc, qb), F32)],
        compiler_params=_cparams("arbitrary", "arbitrary"),
        name="nsa_attention",
    )(q, qr, kc, vct, ks, vst, kw, vwt, gates, jnp.asarray(mt, BF16))


def _proj_ln_kernel(a_ref, w_ref, x_ref, lng_ref, lnb_ref, o_ref):
    mix = jnp.dot(a_ref[...], w_ref[...], preferred_element_type=F32)
    o_ref[...] = _layer_norm(DEEPNORM_ALPHA * x_ref[...] + mix, lng_ref[...], lnb_ref[...])


def proj_ln(a, w, x, ln_g, ln_b, tq=512):
    N, D = x.shape
    K = a.shape[1]
    row = lambda v: v.reshape(1, -1).astype(F32)
    const = lambda shape: pl.BlockSpec(shape, lambda i: (0,) * len(shape))
    return pl.pallas_call(
        _proj_ln_kernel,
        grid=(N // tq,),
        in_specs=[pl.BlockSpec((tq, K), lambda i: (i, 0)), const((K, D)), pl.BlockSpec((tq, D), lambda i: (i, 0)),
                  const((1, D)), const((1, D))],
        out_specs=pl.BlockSpec((tq, D), lambda i: (i, 0)),
        out_shape=jax.ShapeDtypeStruct((N, D), F32),
        compiler_params=_cparams("arbitrary"),
        name="proj_ln",
    )(a, w.astype(BF16), x, row(ln_g), row(ln_b))


def nsa_layer(x, w_in, cmp_pos_k, cmp_w1_k, cmp_w2_k, cmp_pos_v, cmp_w1_v, cmp_w2_v, w_out, ln_g, ln_b):
    B, T, D = x.shape
    q, qr, kc, vc, ks, kw, vst, vwt, gates = nsa_project(x, w_in)
    kc_cmp = nsa_compress(kc, cmp_pos_k, cmp_w1_k, cmp_w2_k)
    vc_cmp = nsa_compress(vc, cmp_pos_v, cmp_w1_v, cmp_w2_v)
    o = nsa_attention(q, qr, kc_cmp, jnp.swapaxes(vc_cmp, 1, 2), ks, vst, kw, vwt, gates)
    return proj_ln(o.reshape(B * T, -1), w_out, x.reshape(B * T, D), ln_g, ln_b)


def kernel(x, ab_w_in, ab_pool_mix, ab_pool_scale, ab_conv_w, ab_conv_b, ab_conv_ln_g, ab_conv_ln_b, ab_w_out, nsa_w_in, nsa_cmp_pos_k, nsa_cmp_w1_k, nsa_cmp_w2_k, nsa_cmp_pos_v, nsa_cmp_w1_v, nsa_cmp_w2_v, nsa_w_out, ln_mix_g, ln_mix_b, ln_ffn_g, ln_ffn_b, moe_w_router, moe_b_router, moe_w_gate, moe_w_up, moe_w_down, moe_ws_gate, moe_ws_up, moe_ws_down):
    B, T, D = x.shape
    moe = lambda h, i: moe_layer(h, moe_w_router[i], moe_b_router[i], moe_w_gate[i], moe_w_up[i], moe_w_down[i],
                                 moe_ws_gate[i], moe_ws_up[i], moe_ws_down[i], ln_ffn_g[i], ln_ffn_b[i])
    h = ab_mixer(x, ab_w_in[0], ab_pool_mix[0], ab_pool_scale[0], ab_conv_w[0], ab_conv_b[0], ab_conv_ln_g[0],
                 ab_conv_ln_b[0], ab_w_out[0], ln_mix_g[0], ln_mix_b[0])
    h = moe(h.reshape(B * T, D), 0)
    h = nsa_layer(h.reshape(B, T, D), nsa_w_in[0], nsa_cmp_pos_k[0], nsa_cmp_w1_k[0], nsa_cmp_w2_k[0],
                  nsa_cmp_pos_v[0], nsa_cmp_w1_v[0], nsa_cmp_w2_v[0], nsa_w_out[0], ln_mix_g[1], ln_mix_b[1])
    h = moe(h, 1)
    return h.reshape(B, T, D)
```

```python
import functools

import jax
import jax.numpy as jnp
import numpy as np
from jax import lax
from jax.experimental import pallas as pl
from jax.experimental.pallas import tpu as pltpu

F32 = jnp.float32
BF16 = jnp.bfloat16
I32 = jnp.int32

D_MODEL = 1024
DEPTH = 2
DEEPNORM_ALPHA = (2.0 * DEPTH) ** 0.25
LN_EPS = 1e-5

POOL_WIDTH = 512
POOL_WINDOWS = (2, 4, 8, 16)
POOL_GROUP = 128
CONV_CH = 512
CONV_TAPS = 31
AB_IN = POOL_WIDTH + 2 * CONV_CH
HALO = 32
CONV_ROWS = 32

N_HEADS = 16
HEAD_DIM = 64
N_KV_GROUPS = 4
HEADS_PER_GROUP = 4
KV_WIDTH = 256
CMP_BLOCK = 32
CMP_STRIDE = 16
CMP_HIDDEN = 128
SLC_BLOCK = 64
N_SELECT = 16
WINDOW = 512
N_BRANCH = 3
ROPE_THETA = 500000.0
ROPE_DIM = 16
SLC_OVERLAP_W = (1.0, 2.0, 2.0, 2.0, 1.0)

N_EXPERTS = 256
TOP_K = 8
N_EXPERT_GROUPS = 8
TOPK_GROUPS = 4
EXPERT_FF = 256
SHARED_FF = 256
ROUTED_SCALE = 2.5
EXPERT_ROWS = 128
MOE_TQ = 256
SLOT_ROWS = 8
CHUNK_ROWS = 16

VMEM_LIMIT = 56 * 1024 * 1024
NEG_INF = float("-inf")


def _cparams(*sem):
    return pltpu.CompilerParams(dimension_semantics=sem, vmem_limit_bytes=VMEM_LIMIT)


def _layer_norm(y, g, b):
    mu = jnp.mean(y, axis=-1, keepdims=True)
    d = y - mu
    var = jnp.mean(d * d, axis=-1, keepdims=True)
    return d * lax.rsqrt(var + LN_EPS) * g + b


def _silu(v):
    return v * jax.nn.sigmoid(v)


def _ab_mixer_kernel(x_ref, win_ref, pmix_ref, pscale_ref, cw_ref, cb_ref, clg_ref, clb_ref,
                     wout_ref, lng_ref, lnb_ref, o_ref, eu_ref, ev_ref, cat_ref, *, tq):
    ti = pl.program_id(1)

    @pl.when(ti == 0)
    def _():
        eu_ref[0:HALO, :] = jnp.zeros((HALO, POOL_WIDTH), F32)
        ev_ref[0:HALO, :] = jnp.zeros((HALO, CONV_CH), F32)

    @pl.when(ti > 0)
    def _():
        eu_ref[0:HALO, :] = eu_ref[tq:tq + HALO, :]
        ev_ref[0:HALO, :] = ev_ref[tq:tq + HALO, :]

    x = x_ref[...]
    h = jnp.dot(x.astype(BF16), win_ref[...], preferred_element_type=F32)
    u = h[:, :POOL_WIDTH]
    eu_ref[HALO:HALO + tq, :] = u
    ev_ref[HALO:HALO + tq, :] = h[:, POOL_WIDTH:POOL_WIDTH + CONV_CH] * jax.nn.sigmoid(
        h[:, POOL_WIDTH + CONV_CH:])

    pos1 = ti * tq + lax.broadcasted_iota(I32, (tq, 1), 0) + 1
    for gi, w in enumerate(POOL_WINDOWS):
        lo, hi = gi * POOL_GROUP, (gi + 1) * POOL_GROUP
        ug = u[:, lo:hi]
        s = ug
        for j in range(1, w):
            s = s + eu_ref[HALO - j:HALO - j + tq, lo:hi]
        cnt = jnp.minimum(pos1, w).astype(F32)
        p = s / cnt - ug
        a = jnp.dot(p.astype(BF16), pmix_ref[gi], preferred_element_type=F32)
        cat_ref[:, lo:hi] = (a * pscale_ref[:, lo:hi]).astype(BF16)

    for r in range(tq // CONV_ROWS):
        base = r * CONV_ROWS + HALO - (CONV_TAPS - 1)
        acc = jnp.broadcast_to(cb_ref[...], (CONV_ROWS, CONV_CH))
        for k in range(CONV_TAPS):
            acc = acc + cw_ref[k:k + 1, :] * ev_ref[base + k:base + k + CONV_ROWS, :]
        c = _silu(_layer_norm(acc, clg_ref[...], clb_ref[...]))
        cat_ref[r * CONV_ROWS:(r + 1) * CONV_ROWS, POOL_WIDTH:] = c.astype(BF16)

    mix = jnp.dot(cat_ref[...], wout_ref[...], preferred_element_type=F32)
    o_ref[...] = _layer_norm(DEEPNORM_ALPHA * x + mix, lng_ref[...], lnb_ref[...])


def ab_mixer(x, w_in, pool_mix, pool_scale, conv_w, conv_b, conv_ln_g, conv_ln_b, w_out, ln_g, ln_b,
             tq=256):
    B, T, D = x.shape
    cw = jnp.pad(conv_w.reshape(CONV_TAPS, CONV_CH), ((0, 32 - CONV_TAPS), (0, 0)))
    row = lambda v: v.reshape(1, -1).astype(F32)
    const = lambda shape: pl.BlockSpec(shape, lambda b, t: (0,) * len(shape))
    return pl.pallas_call(
        functools.partial(_ab_mixer_kernel, tq=tq),
        grid=(B, T // tq),
        in_specs=[
            pl.BlockSpec((None, tq, D), lambda b, t: (b, t, 0)),
            const((D, AB_IN)), const((4, POOL_GROUP, POOL_GROUP)), const((1, POOL_WIDTH)),
            const((32, CONV_CH)), const((1, CONV_CH)), const((1, CONV_CH)), const((1, CONV_CH)),
            const((D, D)), const((1, D)), const((1, D)),
        ],
        out_specs=pl.BlockSpec((None, tq, D), lambda b, t: (b, t, 0)),
        out_shape=jax.ShapeDtypeStruct((B, T, D), F32),
        scratch_shapes=[pltpu.VMEM((HALO + tq, POOL_WIDTH), F32),
                        pltpu.VMEM((HALO + tq, CONV_CH), F32),
                        pltpu.VMEM((tq, D), BF16)],
        compiler_params=_cparams("arbitrary", "arbitrary"),
        name="ab_mixer",
    )(x, w_in.astype(BF16), pool_mix.astype(BF16), row(pool_scale), cw, row(conv_b),
      row(conv_ln_g), row(conv_ln_b), w_out.astype(BF16), row(ln_g), row(ln_b))


def _first_index(hit_src, m, iota, size):
    return jnp.min(jnp.where(hit_src == m, iota, size), axis=0, keepdims=True)


def _router_kernel(x_ref, wh_ref, wl_ref, b_ref, tri_ref, idx_ref, wt_ref, lrank_ref, cnt_ref, *, tq):
    x = x_ref[...]
    xh = x.astype(BF16)
    xl = (x - xh.astype(F32)).astype(BF16)
    nt = (((1,), (1,)), ((), ()))
    wh, wl = wh_ref[...], wl_ref[...]
    logits = (lax.dot_general(wh, xh, nt, preferred_element_type=F32)
              + lax.dot_general(wh, xl, nt, preferred_element_type=F32)
              + lax.dot_general(wl, xh, nt, preferred_element_type=F32))
    scores = jax.nn.sigmoid(logits)
    biased = scores + b_ref[...]

    per_group = N_EXPERTS // N_EXPERT_GROUPS
    io_g = lax.broadcasted_iota(I32, (per_group, tq), 0)
    gs_rows = []
    for g in range(N_EXPERT_GROUPS):
        sg = biased[g * per_group:(g + 1) * per_group, :]
        m1 = jnp.max(sg, axis=0, keepdims=True)
        i1 = _first_index(sg, m1, io_g, per_group)
        m2 = jnp.max(jnp.where(io_g == i1, NEG_INF, sg), axis=0, keepdims=True)
        gs_rows.append(m1 + m2)
    gs = jnp.concatenate(gs_rows, axis=0)

    io_ng = lax.broadcasted_iota(I32, (N_EXPERT_GROUPS, tq), 0)
    grp_sel = jnp.zeros((N_EXPERT_GROUPS, tq), jnp.bool_)
    for _ in range(TOPK_GROUPS):
        m = jnp.max(gs, axis=0, keepdims=True)
        hit = io_ng == _first_index(gs, m, io_ng, N_EXPERT_GROUPS)
        grp_sel = grp_sel | hit
        gs = jnp.where(hit, NEG_INF, gs)
    exp_mask = jnp.concatenate(
        [jnp.broadcast_to(grp_sel[g:g + 1, :], (per_group, tq)) for g in range(N_EXPERT_GROUPS)], axis=0)

    masked = jnp.where(exp_mask, biased, NEG_INF)
    io_e = lax.broadcasted_iota(I32, (N_EXPERTS, tq), 0)
    idx_rows, w_rows = [], []
    onehot = jnp.zeros((N_EXPERTS, tq), F32)
    for _ in range(TOP_K):
        m = jnp.max(masked, axis=0, keepdims=True)
        first = _first_index(masked, m, io_e, N_EXPERTS)
        hit = io_e == first
        idx_rows.append(first)
        w_rows.append(jnp.sum(jnp.where(hit, scores, 0.0), axis=0, keepdims=True))
        onehot = onehot + hit.astype(F32)
        masked = jnp.where(hit, NEG_INF, masked)
    w = jnp.concatenate(w_rows, axis=0)
    wt_ref[...] = w / jnp.sum(w, axis=0, keepdims=True) * ROUTED_SCALE
    idx_ref[...] = jnp.concatenate(idx_rows, axis=0)

    oh = onehot.astype(BF16)
    before = jnp.dot(oh, tri_ref[...], preferred_element_type=F32)
    lrank_ref[...] = jnp.concatenate(
        [jnp.sum(jnp.where(io_e == r, before, 0.0), axis=0, keepdims=True) for r in idx_rows],
        axis=0).astype(I32)
    cnt_ref[...] = lax.dot_general(jnp.ones((8, tq), BF16), oh, nt, preferred_element_type=F32).astype(I32)


def moe_router(x, w_router, b_router, tq):
    N, D = x.shape
    wt = w_router.T
    wh = wt.astype(BF16)
    wl = (wt - wh.astype(F32)).astype(BF16)
    tri = (np.arange(tq)[:, None] < np.arange(tq)[None, :]).astype(np.float32)
    const = lambda shape: pl.BlockSpec(shape, lambda i: (0,) * len(shape))
    tok = lambda rows: pl.BlockSpec((rows, tq), lambda i: (0, i))
    return pl.pallas_call(
        functools.partial(_router_kernel, tq=tq),
        grid=(N // tq,),
        in_specs=[pl.BlockSpec((tq, D), lambda i: (i, 0)), const((N_EXPERTS, D)), const((N_EXPERTS, D)),
                  const((N_EXPERTS, 1)), const((tq, tq))],
        out_specs=[tok(TOP_K), tok(TOP_K), tok(TOP_K), pl.BlockSpec((None, 8, N_EXPERTS), lambda i: (i, 0, 0))],
        out_shape=[jax.ShapeDtypeStruct((TOP_K, N), I32), jax.ShapeDtypeStruct((TOP_K, N), F32),
                   jax.ShapeDtypeStruct((TOP_K, N), I32), jax.ShapeDtypeStruct((N // tq, 8, N_EXPERTS), I32)],
        compiler_params=_cparams("arbitrary"),
        name="moe_router",
    )(x, wh, wl, b_router.reshape(N_EXPERTS, 1).astype(F32), jnp.asarray(tri, BF16))


def _pack_pairs(v):
    half = v.shape[1] // 2
    hi = lax.bitcast_convert_type(v[:, :half], I32)
    lo = lax.bitcast_convert_type(v[:, half:], I32)
    return (hi & jnp.int32(-65536)) | lax.shift_right_logical(lo, jnp.int32(16))


def _unpack_pairs(p):
    hi = lax.bitcast_convert_type(p & jnp.int32(-65536), F32)
    lo = lax.bitcast_convert_type(lax.shift_left(p, jnp.int32(16)), F32)
    return jnp.concatenate([hi, lo], axis=1).astype(BF16)


def _round_bf16(v):
    return v.astype(BF16).astype(F32)


def _slot_copies(cnt_ref, off_ref, hbm, buf, sem, to_hbm):
    def issue(e, carry):
        n_small, n_big = carry
        c = cnt_ref[0, 0, e]
        off = pl.multiple_of(off_ref[0, 0, e], SLOT_ROWS)
        small = (c > 0) & (c <= SLOT_ROWS)
        big = c > SLOT_ROWS

        @pl.when(small)
        def _():
            pair = (buf.at[pl.ds(0, SLOT_ROWS), e, :], hbm.at[pl.ds(off, SLOT_ROWS), :])
            pltpu.make_async_copy(*(pair if to_hbm else pair[::-1]), sem).start()

        @pl.when(big)
        def _():
            pair = (buf.at[:, e, :], hbm.at[pl.ds(off, CHUNK_ROWS), :])
            pltpu.make_async_copy(*(pair if to_hbm else pair[::-1]), sem).start()
        return n_small + small.astype(I32), n_big + big.astype(I32)
    return lax.fori_loop(0, N_EXPERTS, issue, (0, 0))


def _slot_waits(n_small, n_big, hbm, buf, sem, to_hbm):
    def drain(rows):
        def body(_, c):
            pair = (buf.at[pl.ds(0, rows), 0, :], hbm.at[pl.ds(0, rows), :])
            pltpu.make_async_copy(*(pair if to_hbm else pair[::-1]), sem).wait()
            return c
        return body
    lax.fori_loop(0, n_small, drain(SLOT_ROWS), 0)
    lax.fori_loop(0, n_big, drain(CHUNK_ROWS), 0)


def _dispatch_kernel(cnt_ref, off_ref, x_ref, idx_ref, lrk_ref, xs_hbm, xbuf, obuf, sem, *, tq):
    xb = x_ref[...].astype(BF16)
    io_e = lax.broadcasted_iota(I32, (N_EXPERTS, tq), 0)
    lrk_at = jnp.full((N_EXPERTS, tq), -1, I32)
    for k in range(TOP_K):
        lrk_at = jnp.where(io_e == idx_ref[k:k + 1, :], lrk_ref[k:k + 1, :], lrk_at)
    for r in range(CHUNK_ROWS):
        pick = (lrk_at == r).astype(BF16)
        xbuf[r] = _pack_pairs(jnp.dot(pick, xb, preferred_element_type=F32))
    n_small, n_big = _slot_copies(cnt_ref, off_ref, xs_hbm, xbuf, sem, True)
    _slot_waits(n_small, n_big, xs_hbm, xbuf, sem, True)

    io_r = lax.broadcasted_iota(I32, (SLOT_ROWS, tq), 0)

    def overflow(e, c):
        n_units = (cnt_ref[0, 0, e] + SLOT_ROWS - 1) // SLOT_ROWS
        off = pl.multiple_of(off_ref[0, 0, e], SLOT_ROWS)

        def extra(j, c2):
            rank_e = jnp.full((1, tq), -1, I32)
            for k in range(TOP_K):
                rank_e = jnp.where(idx_ref[k:k + 1, :] == e, lrk_ref[k:k + 1, :], rank_e)
            pick = (io_r + j * SLOT_ROWS == rank_e).astype(BF16)
            obuf[...] = _pack_pairs(jnp.dot(pick, xb, preferred_element_type=F32))
            cp = pltpu.make_async_copy(obuf, xs_hbm.at[pl.ds(off + j * SLOT_ROWS, SLOT_ROWS), :], sem)
            cp.start()
            cp.wait()
            return c2
        return lax.fori_loop(CHUNK_ROWS // SLOT_ROWS, n_units, extra, c)
    lax.fori_loop(0, N_EXPERTS, overflow, 0)


def moe_dispatch(x, idx, lrank, tile_cnt, tile_off, n_rows, tq):
    N, D = x.shape
    n_tiles = N // tq
    smem = lambda: pl.BlockSpec((1, 1, N_EXPERTS), lambda i: (i, 0, 0), memory_space=pltpu.SMEM)
    return pl.pallas_call(
        functools.partial(_dispatch_kernel, tq=tq),
        grid=(n_tiles,),
        in_specs=[smem(), smem(), pl.BlockSpec((tq, D), lambda i: (i, 0)),
                  pl.BlockSpec((TOP_K, tq), lambda i: (0, i)), pl.BlockSpec((TOP_K, tq), lambda i: (0, i))],
        out_specs=pl.BlockSpec(memory_space=pl.ANY),
        out_shape=jax.ShapeDtypeStruct((n_rows, D // 2), I32),
        scratch_shapes=[pltpu.VMEM((CHUNK_ROWS, N_EXPERTS, D // 2), I32), pltpu.VMEM((SLOT_ROWS, D // 2), I32),
                        pltpu.SemaphoreType.DMA(())],
        compiler_params=_cparams("arbitrary"),
        name="moe_dispatch",
    )(tile_cnt.reshape(n_tiles, 1, N_EXPERTS), tile_off.reshape(n_tiles, 1, N_EXPERTS), x, idx, lrank)


def _expert_kernel(bexp_ref, nused_ref, nvalid_ref, x_ref, wg_ref, wu_ref, wd_ref, y_ref, wgb, wub, wdb):
    b = pl.program_id(0)
    used = b < nused_ref[0]
    fresh = (b == 0) | (bexp_ref[b] != bexp_ref[jnp.maximum(b - 1, 0)])

    @pl.when(used & fresh)
    def _():
        wgb[...] = wg_ref[...].astype(BF16)
        wub[...] = wu_ref[...].astype(BF16)
        wdb[...] = wd_ref[...].astype(BF16)

    @pl.when(used)
    def _():
        rows = lax.broadcasted_iota(I32, x_ref.shape, 0)
        xb = _unpack_pairs(jnp.where(rows < nvalid_ref[b], x_ref[...], 0))
        g = jnp.dot(xb, wgb[...], preferred_element_type=F32)
        u = jnp.dot(xb, wub[...], preferred_element_type=F32)
        hid = (_silu(g) * u).astype(BF16)
        y_ref[...] = _pack_pairs(_round_bf16(jnp.dot(hid, wdb[...], preferred_element_type=F32)))

    @pl.when(jnp.logical_not(used))
    def _():
        y_ref[...] = jnp.zeros_like(y_ref)


def moe_experts(xs, block_exp, n_used, n_valid, w_gate, w_up, w_down, layer):
    m = EXPERT_ROWS
    n_blocks = xs.shape[0] // m - 1
    D = xs.shape[1] * 2
    wspec = lambda r, c: pl.BlockSpec((None, None, r, c), lambda b, be, nu, nv: (layer, be[b], 0, 0))
    rows = lambda: pl.BlockSpec((m, D // 2), lambda b, be, nu, nv: (jnp.minimum(b, nu[0]), 0))
    grid_spec = pltpu.PrefetchScalarGridSpec(
        num_scalar_prefetch=3,
        grid=(n_blocks,),
        in_specs=[rows(), wspec(D, EXPERT_FF), wspec(D, EXPERT_FF), wspec(EXPERT_FF, D)],
        out_specs=rows(),
        scratch_shapes=[pltpu.VMEM((D, EXPERT_FF), BF16), pltpu.VMEM((D, EXPERT_FF), BF16),
                        pltpu.VMEM((EXPERT_FF, D), BF16)],
    )
    return pl.pallas_call(
        _expert_kernel,
        grid_spec=grid_spec,
        out_shape=jax.ShapeDtypeStruct(xs.shape, I32),
        compiler_params=_cparams("arbitrary"),
        name="moe_experts",
    )(block_exp, n_used, n_valid, xs, w_gate, w_up, w_down)


def _combine_kernel(cnt_ref, off_ref, y_hbm, idx_ref, lrk_ref, w_ref, x_ref, wsg_ref, wsu_ref, wsd_ref,
                    lng_ref, lnb_ref, o_ref, ybuf, obuf, acc_ref, sem, *, tq):
    @pl.when(pl.program_id(0) == 0)
    def _():
        ybuf[...] = jnp.zeros_like(ybuf)

    n_small, n_big = _slot_copies(cnt_ref, off_ref, y_hbm, ybuf, sem, False)

    x = x_ref[...]
    xb = x.astype(BF16)
    g = jnp.dot(xb, wsg_ref[...], preferred_element_type=F32)
    u = jnp.dot(xb, wsu_ref[...], preferred_element_type=F32)
    acc_ref[...] = jnp.dot((_silu(g) * u).astype(BF16), wsd_ref[...], preferred_element_type=F32)

    io_e = lax.broadcasted_iota(I32, (tq, N_EXPERTS), 1)
    idx, lrk, w = idx_ref[...], lrk_ref[...], w_ref[...]
    w_at = jnp.zeros((tq, N_EXPERTS), F32)
    lrk_at = jnp.full((tq, N_EXPERTS), -1, I32)
    for k in range(TOP_K):
        hit = io_e == idx[:, k:k + 1]
        w_at = jnp.where(hit, w[:, k:k + 1], w_at)
        lrk_at = jnp.where(hit, lrk[:, k:k + 1], lrk_at)

    _slot_waits(n_small, n_big, y_hbm, ybuf, sem, False)

    group = 4
    for r0 in range(0, CHUNK_ROWS, group):
        wc = jnp.concatenate([jnp.where(lrk_at == r, w_at, 0.0).astype(BF16) for r in range(r0, r0 + group)],
                             axis=1)
        yb = jnp.concatenate([_unpack_pairs(ybuf[r]) for r in range(r0, r0 + group)], axis=0)
        acc_ref[...] += jnp.dot(wc, yb, preferred_element_type=F32)

    io_r = lax.broadcasted_iota(I32, (tq, SLOT_ROWS), 1)

    def overflow(e, c):
        n_units = (cnt_ref[0, 0, e] + SLOT_ROWS - 1) // SLOT_ROWS
        off = pl.multiple_of(off_ref[0, 0, e], SLOT_ROWS)

        def extra(j, c2):
            cp = pltpu.make_async_copy(y_hbm.at[pl.ds(off + j * SLOT_ROWS, SLOT_ROWS), :], obuf, sem)
            cp.start()
            w_e = jnp.zeros((tq, 1), F32)
            rank_e = jnp.full((tq, 1), -1, I32)
            for k in range(TOP_K):
                hit = idx[:, k:k + 1] == e
                w_e = jnp.where(hit, w[:, k:k + 1], w_e)
                rank_e = jnp.where(hit, lrk[:, k:k + 1], rank_e)
            wc = jnp.where(io_r + j * SLOT_ROWS == rank_e, w_e, 0.0).astype(BF16)
            cp.wait()
            acc_ref[...] += jnp.dot(wc, _unpack_pairs(obuf[...]), preferred_element_type=F32)
            return c2
        return lax.fori_loop(CHUNK_ROWS // SLOT_ROWS, n_units, extra, c)
    lax.fori_loop(0, N_EXPERTS, overflow, 0)

    o_ref[...] = _layer_norm(DEEPNORM_ALPHA * x + acc_ref[...], lng_ref[...], lnb_ref[...])


def moe_combine(x, ys, idx_tok, lrank_tok, w_tok, tile_cnt, tile_off, ws_gate, ws_up, ws_down, ln_g, ln_b, tq):
    N, D = x.shape
    n_tiles = N // tq
    row = lambda v: v.reshape(1, -1).astype(F32)
    const = lambda shape: pl.BlockSpec(shape, lambda i: (0,) * len(shape))
    smem = lambda: pl.BlockSpec((1, 1, N_EXPERTS), lambda i: (i, 0, 0), memory_space=pltpu.SMEM)
    tokm = lambda: pl.BlockSpec((tq, TOP_K), lambda i: (i, 0))
    return pl.pallas_call(
        functools.partial(_combine_kernel, tq=tq),
        grid=(n_tiles,),
        in_specs=[smem(), smem(), pl.BlockSpec(memory_space=pl.ANY), tokm(), tokm(), tokm(),
                  pl.BlockSpec((tq, D), lambda i: (i, 0)),
                  const((D, SHARED_FF)), const((D, SHARED_FF)), const((SHARED_FF, D)), const((1, D)), const((1, D))],
        out_specs=pl.BlockSpec((tq, D), lambda i: (i, 0)),
        out_shape=jax.ShapeDtypeStruct((N, D), F32),
        scratch_shapes=[pltpu.VMEM((CHUNK_ROWS, N_EXPERTS, D // 2), I32), pltpu.VMEM((SLOT_ROWS, D // 2), I32),
                        pltpu.VMEM((tq, D), F32), pltpu.SemaphoreType.DMA(())],
        compiler_params=_cparams("arbitrary"),
        name="moe_combine",
    )(tile_cnt.reshape(n_tiles, 1, N_EXPERTS), tile_off.reshape(n_tiles, 1, N_EXPERTS), ys, idx_tok, lrank_tok,
      w_tok, x, ws_gate.astype(BF16), ws_up.astype(BF16), ws_down.astype(BF16), row(ln_g), row(ln_b))


def moe_layer(x, layer, w_router, b_router, w_gate, w_up, w_down, ws_gate, ws_up, ws_down, ln_g, ln_b):
    N, D = x.shape
    m = EXPERT_ROWS
    tq = MOE_TQ
    idx, wts, lrank, tile_cnt = moe_router(x, w_router, b_router, tq)
    tile_cnt = tile_cnt[:, 0, :]
    slots = (tile_cnt + SLOT_ROWS - 1) // SLOT_ROWS * SLOT_ROWS
    used = jnp.sum(slots, axis=0)
    padded = (used + m - 1) // m * m
    pend = jnp.cumsum(padded)
    pstart = pend - padded
    tile_off = pstart[None, :] + jnp.cumsum(slots, axis=0) - slots
    n_assign = N * TOP_K
    max_rows = n_assign + (SLOT_ROWS - 1) * min(n_assign, (N // tq) * N_EXPERTS) + N_EXPERTS * (m - 1)
    n_blocks = -(-max_rows // m)
    blk = jnp.arange(n_blocks, dtype=I32)
    block_exp = jnp.minimum(jnp.searchsorted(pend, blk * m, side="right"), N_EXPERTS - 1).astype(I32)
    n_valid = jnp.clip(pstart[block_exp] + used[block_exp] - blk * m, 0, m).astype(I32)
    n_used = (pend[-1:] // m).astype(I32)
    xs = moe_dispatch(x, idx, lrank, tile_cnt, tile_off, (n_blocks + 1) * m, tq)
    ys = moe_experts(xs, block_exp, n_used, n_valid, w_gate, w_up, w_down, layer)
    return moe_combine(x, ys, idx.T, lrank.T, wts.T, tile_cnt, tile_off, ws_gate, ws_up, ws_down, ln_g, ln_b, tq)


Q_TILE = 128
SEL_KEYS = 512
WIN_TILE = 128
WIN_KEYS = WINDOW + Q_TILE
PROJ_TQ = SEL_KEYS
KS_WIDTH = N_KV_GROUPS * 128
VS_ROWS = HEAD_DIM + 16
NSA_TOK_COLS = N_HEADS * HEAD_DIM + 3 * KV_WIDTH + KS_WIDTH
NSA_FEAT_ROWS = 2 * KV_WIDTH + N_HEADS * N_BRANCH
NT_DIMS = (((1,), (1,)), ((), ()))
LOG2E = 1.4426950408889634
MASK_BIAS = -1e30


def _rotary(v, cos, sin, lo_mask):
    parts = []
    for c in range(v.shape[1] // 128):
        blk = v[:, c * 128:(c + 1) * 128]
        nxt = pltpu.roll(blk, 128 - ROPE_DIM // 2, axis=1)
        prv = pltpu.roll(blk, ROPE_DIM // 2, axis=1)
        parts.append(blk * cos + jnp.where(lo_mask, nxt, prv) * sin)
    return jnp.concatenate(parts, axis=1)


def _nsa_proj_kernel(x_ref, wtok_ref, wfeat_ref, cos_ref, sin_ref, cosk_ref, sink_ref, q_ref, qr_ref, kc_ref,
                     vc_ref, ks_ref, kw_ref, vs_ref, vw_ref, g_ref, *, tq):
    xb = x_ref[...].astype(BF16)
    h = jnp.dot(xb, wtok_ref[...], preferred_element_type=F32)
    cos, sin = cos_ref[...], sin_ref[...]
    lane = lax.broadcasted_iota(I32, (tq, 128), 1)
    lo_mask = (lane % HEAD_DIM) < ROPE_DIM // 2
    nq = N_HEADS * HEAD_DIM
    q = h[:, :nq] * (HEAD_DIM ** -0.5)
    q_ref[...] = q.astype(BF16)
    qr_ref[...] = (_rotary(q, cos, sin, lo_mask) * LOG2E).astype(BF16)
    c0 = nq
    kc_ref[...] = h[:, c0:c0 + KV_WIDTH].astype(BF16)
    vc_ref[...] = h[:, c0 + KV_WIDTH:c0 + 2 * KV_WIDTH].astype(BF16)
    c0 += 2 * KV_WIDTH
    ks = _rotary(h[:, c0:c0 + N_KV_GROUPS * 128], cosk_ref[...], sink_ref[...], lo_mask)
    blk = lax.broadcasted_iota(I32, (tq, 128), 0) % SEL_KEYS // SLC_BLOCK
    onehot = (lane - HEAD_DIM == blk).astype(F32)
    ks_ref[...] = (ks + jnp.concatenate([onehot] * N_KV_GROUPS, axis=1)).astype(BF16)
    c0 += N_KV_GROUPS * 128
    kw_ref[...] = _rotary(h[:, c0:c0 + KV_WIDTH], cos, sin, lo_mask).astype(BF16)
    hf = lax.dot_general(wfeat_ref[...], xb, NT_DIMS, preferred_element_type=F32)
    ones_row = (lax.broadcasted_iota(I32, (VS_ROWS - HEAD_DIM, tq), 0) == 0).astype(BF16)
    for g in range(N_KV_GROUPS):
        vs_ref[g, 0:HEAD_DIM, :] = hf[g * HEAD_DIM:(g + 1) * HEAD_DIM].astype(BF16)
        vs_ref[g, HEAD_DIM:VS_ROWS, :] = ones_row
    for j in range(tq // WIN_TILE):
        vw_ref[j] = hf[KV_WIDTH:2 * KV_WIDTH, j * WIN_TILE:(j + 1) * WIN_TILE].astype(BF16)
    g_ref[...] = jax.nn.sigmoid(hf[2 * KV_WIDTH:])


def nsa_project(x, w_in, tq=PROJ_TQ):
    B, T, D = x.shape
    nq = N_HEADS * HEAD_DIM
    q, kc, vc, ks, vs, kw, vw, g = jnp.split(w_in, [nq + i * KV_WIDTH for i in range(7)], axis=1)
    ks_wide = jnp.pad(ks.reshape(D, N_KV_GROUPS, HEAD_DIM), ((0, 0), (0, 0), (0, 128 - HEAD_DIM)))
    wtok = jnp.concatenate([q, kc, vc, ks_wide.reshape(D, KS_WIDTH), kw], axis=1).astype(BF16)
    wfeat = jnp.concatenate([vs, vw, g], axis=1).T.astype(BF16)
    inv_freq = jnp.power(jnp.float32(ROPE_THETA), -jnp.arange(0, ROPE_DIM, 2, dtype=F32) / ROPE_DIM)
    ang = jnp.arange(T).astype(F32)[:, None] * inv_freq[None, :]
    ones = jnp.ones((T, HEAD_DIM - ROPE_DIM), F32)
    cos_h = jnp.concatenate([jnp.cos(ang), jnp.cos(ang), ones], axis=1)
    sin_h = jnp.concatenate([-jnp.sin(ang), jnp.sin(ang), 0.0 * ones], axis=1)
    ident = jnp.ones((T, 128 - HEAD_DIM), F32)
    tables = [jnp.concatenate([cos_h, cos_h], axis=1), jnp.concatenate([sin_h, sin_h], axis=1),
              jnp.concatenate([cos_h, ident], axis=1), jnp.concatenate([sin_h, 0.0 * ident], axis=1)]
    nt = T // tq
    tokm = lambda w: pl.BlockSpec((None, tq, w), lambda b, t: (b, t, 0))
    const = lambda shape: pl.BlockSpec(shape, lambda b, t: (0,) * len(shape))
    table = lambda: pl.BlockSpec((tq, 128), lambda b, t: (t, 0))
    bf = lambda *shape: jax.ShapeDtypeStruct(shape, BF16)
    return pl.pallas_call(
        functools.partial(_nsa_proj_kernel, tq=tq),
        grid=(B, nt),
        in_specs=[tokm(D), const((D, NSA_TOK_COLS)), const((NSA_FEAT_ROWS, D)), table(), table(), table(), table()],
        out_specs=[tokm(nq), tokm(nq), tokm(KV_WIDTH), tokm(KV_WIDTH), tokm(KS_WIDTH), tokm(KV_WIDTH),
                   pl.BlockSpec((None, None, N_KV_GROUPS, VS_ROWS, tq), lambda b, t: (b, t, 0, 0, 0)),
                   pl.BlockSpec((None, tq // WIN_TILE, KV_WIDTH, WIN_TILE), lambda b, t: (b, t, 0, 0)),
                   pl.BlockSpec((None, N_HEADS * N_BRANCH, tq), lambda b, t: (b, 0, t))],
        out_shape=[bf(B, T, nq), bf(B, T, nq), bf(B, T, KV_WIDTH), bf(B, T, KV_WIDTH), bf(B, T, KS_WIDTH),
                   bf(B, T, KV_WIDTH), bf(B, T // tq, N_KV_GROUPS, VS_ROWS, tq),
                   bf(B, T // WIN_TILE, KV_WIDTH, WIN_TILE),
                   jax.ShapeDtypeStruct((B, N_HEADS * N_BRANCH, T), F32)],
        compiler_params=_cparams("arbitrary", "arbitrary"),
        name="nsa_project",
    )(x, wtok, wfeat, *tables)


def _compress_kernel(x_ref, pos_ref, w1_ref, w2_ref, o_ref):
    blocks = (x_ref[...].astype(F32) + pos_ref[...]).astype(BF16)
    pre = jnp.dot(blocks, w1_ref[...], preferred_element_type=F32)
    hid = jax.nn.gelu(pre, approximate=True)
    o_ref[...] = jnp.dot(hid.astype(BF16), w2_ref[...], preferred_element_type=F32).astype(BF16)


def nsa_compress(kv, pos_emb, w1, w2, tr=512):
    B, T, _ = kv.shape
    nch = T // CMP_STRIDE
    ch = kv.reshape(B, nch, CMP_STRIDE, N_KV_GROUPS, HEAD_DIM)
    blocks = jnp.concatenate([ch[:, :-1], ch[:, 1:]], axis=2)
    flat = jnp.moveaxis(blocks, 3, 2).reshape(B, nch - 1, N_KV_GROUPS, CMP_BLOCK * HEAD_DIM)
    flat = jnp.pad(flat, ((0, 0), (0, 1), (0, 0), (0, 0))).reshape(B * nch * N_KV_GROUPS, CMP_BLOCK * HEAD_DIM)
    rows = flat.shape[0]
    tr = min(tr, rows)
    const = lambda shape: pl.BlockSpec(shape, lambda i: (0,) * len(shape))
    out = pl.pallas_call(
        _compress_kernel,
        grid=(rows // tr,),
        in_specs=[pl.BlockSpec((tr, CMP_BLOCK * HEAD_DIM), lambda i: (i, 0)), const((1, CMP_BLOCK * HEAD_DIM)),
                  const((CMP_BLOCK * HEAD_DIM, CMP_HIDDEN)), const((CMP_HIDDEN, HEAD_DIM))],
        out_specs=pl.BlockSpec((tr, HEAD_DIM), lambda i: (i, 0)),
        out_shape=jax.ShapeDtypeStruct((rows, HEAD_DIM), BF16),
        compiler_params=_cparams("arbitrary"),
        name="nsa_compress",
    )(flat, pos_emb.reshape(1, -1).astype(F32), w1.astype(BF16), w2.astype(BF16))
    return out.reshape(B, nch, N_KV_GROUPS * HEAD_DIM)


def _softmax_keys(s, mask, base2):
    s = jnp.where(mask, s, NEG_INF)
    m = jnp.max(s, axis=0, keepdims=True)
    m = jnp.where((m > NEG_INF) & (m < float("inf")), m, 0.0)
    e = jnp.exp2(s - m) if base2 else jnp.exp(s - m)
    d = jnp.sum(e, axis=0, keepdims=True)
    return e / jnp.where(d > 0, d, 1.0)


def _nsa_attn_kernel(q_ref, qr_ref, kc_ref, vct_ref, ks_ref, vst_ref, kw_ref, vwt_ref, g_ref, mt_ref, o_ref,
                     bias_ref, *, qb):
    t0 = pl.program_id(1) * qb
    hpg = HEADS_PER_GROUP
    t_row = t0 + lax.broadcasted_iota(I32, (1, qb), 1)
    t_all = jnp.concatenate([t_row] * hpg, axis=1)
    q = q_ref[...]
    qr_t = qr_ref[...].astype(F32).T
    gates = g_ref[...]
    n_cmp = kc_ref.shape[0]
    n_slc = mt_ref.shape[0]
    blocks_per_tile = SEL_KEYS // SLC_BLOCK
    o_cmp_all = []
    for g in range(N_KV_GROUPS):
        lanes = slice(g * HEAD_DIM, (g + 1) * HEAD_DIM)
        qg = jnp.concatenate(
            [q[:, (g * hpg + hp) * HEAD_DIM:(g * hpg + hp + 1) * HEAD_DIM] for hp in range(hpg)], axis=0)

        sc = lax.dot_general(kc_ref[:, lanes], qg, NT_DIMS, preferred_element_type=F32)
        cmp_end = lax.broadcasted_iota(I32, (n_cmp, 1), 0) * CMP_STRIDE + (CMP_BLOCK - 1)
        pc = _softmax_keys(sc, cmp_end <= t_all, False)
        o_cmp_all.append(jnp.dot(vct_ref[lanes, :], pc.astype(BF16), preferred_element_type=F32))

        pcs = pc[:, :qb]
        for hp in range(1, hpg):
            pcs = pcs + pc[:, hp * qb:(hp + 1) * qb]
        hi = pcs.astype(BF16)
        r1 = pcs - hi.astype(F32)
        mid = r1.astype(BF16)
        low = (r1 - mid.astype(F32)).astype(BF16)
        mt = mt_ref[...]
        imp = (jnp.dot(mt, hi, preferred_element_type=F32) + jnp.dot(mt, mid, preferred_element_type=F32)
               + jnp.dot(mt, low, preferred_element_type=F32))
        j_io = lax.broadcasted_iota(I32, (n_slc, 1), 0)
        cur = t_row // SLC_BLOCK
        forced = (j_io == 0) | (j_io == cur) | (j_io == cur - 1)
        imp = jnp.where(forced, float("inf"), jnp.where(j_io * SLC_BLOCK <= t_row, imp, NEG_INF))
        sel = jnp.zeros((n_slc, qb), jnp.bool_)
        for _ in range(min(N_SELECT, n_slc)):
            m = jnp.max(imp, axis=0, keepdims=True)
            hit = j_io == jnp.min(jnp.where(imp == m, j_io, n_slc), axis=0, keepdims=True)
            sel = sel | hit
            imp = jnp.where(hit, NEG_INF, imp)
        bias_ref[g] = jnp.where(sel, 0.0, MASK_BIAS)

    q_ts = [jnp.concatenate([qr_t[(g * hpg + hp) * HEAD_DIM:(g * hpg + hp + 1) * HEAD_DIM, :]
                             for hp in range(hpg)], axis=1) for g in range(N_KV_GROUPS)]
    zero_rows = jnp.zeros((128 - HEAD_DIM - blocks_per_tile, hpg * qb), F32)

    def sel_tile(kt, carry, causal):
        k0 = pl.multiple_of(kt * SEL_KEYS, SEL_KEYS)
        b0 = pl.multiple_of(kt * blocks_per_tile, blocks_per_tile)
        out = []
        for g in range(N_KV_GROUPS):
            m_run, acc = carry[g]
            mask_rows = bias_ref[g, pl.ds(b0, blocks_per_tile), :]
            rhs = jnp.concatenate([q_ts[g], jnp.concatenate([mask_rows] * hpg, axis=1), zero_rows], axis=0)
            s = jnp.dot(ks_ref[pl.ds(k0, SEL_KEYS), g * 128:(g + 1) * 128], rhs.astype(BF16),
                        preferred_element_type=F32)
            if causal:
                kpos = k0 + lax.broadcasted_iota(I32, (SEL_KEYS, 1), 0)
                s = jnp.where(kpos <= t_all, s, MASK_BIAS)
            m_new = jnp.maximum(m_run, jnp.max(s, axis=0, keepdims=True))
            p = jnp.exp2(s - m_new)
            pv = jnp.dot(vst_ref[kt, g], p.astype(BF16), preferred_element_type=F32)
            out.append((m_new, jnp.exp2(m_run - m_new) * acc + pv))
        return tuple(out)

    init = tuple((jnp.full((1, hpg * qb), NEG_INF, F32), jnp.zeros((VS_ROWS, hpg * qb), F32))
                 for _ in range(N_KV_GROUPS))
    last = t0 // SEL_KEYS
    carry = lax.fori_loop(0, last, functools.partial(sel_tile, causal=False), init)
    sel_out = sel_tile(last, carry, True)

    heads = []
    for g in range(N_KV_GROUPS):
        lanes = slice(g * HEAD_DIM, (g + 1) * HEAD_DIM)
        q_t = q_ts[g]
        acc = sel_out[g][1]
        l_fin = acc[HEAD_DIM:HEAD_DIM + 1, :]
        o_sel = acc[:HEAD_DIM, :] / jnp.where(l_fin > 0, l_fin, 1.0)

        start = pl.multiple_of(jnp.maximum(t0 - WINDOW, 0), WIN_TILE)
        sw = jnp.dot(kw_ref[pl.ds(start, WIN_KEYS), lanes], q_t.astype(BF16), preferred_element_type=F32)
        dlt = t_all - (start + lax.broadcasted_iota(I32, (WIN_KEYS, 1), 0))
        pw = _softmax_keys(sw, (dlt >= 0) & (dlt < WINDOW), True)
        w0 = start // WIN_TILE
        vwb = jnp.concatenate([vwt_ref[w0 + j, lanes, :] for j in range(WIN_KEYS // WIN_TILE)], axis=1)
        o_win = jnp.dot(vwb, pw.astype(BF16), preferred_element_type=F32)

        o_cmp = o_cmp_all[g]
        for hp in range(hpg):
            r = (g * hpg + hp) * N_BRANCH
            cols = slice(hp * qb, (hp + 1) * qb)
            heads.append(gates[r:r + 1] * o_cmp[:, cols] + gates[r + 1:r + 2] * o_sel[:, cols]
                         + gates[r + 2:r + 3] * o_win[:, cols])
    o_ref[...] = jnp.concatenate(heads, axis=0).T.astype(BF16)


def nsa_attention(q, qr, kc, vct, ks, vst, kw, vwt, gates, qb=Q_TILE):
    B, T, nq = q.shape
    n_cmp = kc.shape[1]
    n_slc = T // SLC_BLOCK
    mt = np.zeros((n_slc, n_cmp), np.float32)
    per = SLC_BLOCK // CMP_STRIDE
    for j in range(n_slc):
        for k, wk in enumerate(SLC_OVERLAP_W):
            n = per * j + k - 1
            if 0 <= n < n_cmp - 1:
                mt[j, n] = wk
    per_b = lambda shape: pl.BlockSpec((None,) + shape, lambda b, i: (b,) + (0,) * len(shape))
    return pl.pallas_call(
        functools.partial(_nsa_attn_kernel, qb=qb),
        grid=(B, T // qb),
        in_specs=[pl.BlockSpec((None, qb, nq), lambda b, i: (b, i, 0)),
                  pl.BlockSpec((None, qb, nq), lambda b, i: (b, i, 0)),
                  per_b((n_cmp, KV_WIDTH)), per_b((KV_WIDTH, n_cmp)),
                  per_b((T, KS_WIDTH)), per_b((T // SEL_KEYS, N_KV_GROUPS, VS_ROWS, SEL_KEYS)),
                  per_b((T, KV_WIDTH)), per_b((T // WIN_TILE, KV_WIDTH, WIN_TILE)),
                  pl.BlockSpec((None, N_HEADS * N_BRANCH, qb), lambda b, i: (b, 0, i)),
                  pl.BlockSpec((n_slc, n_cmp), lambda b, i: (0, 0))],
        out_specs=pl.BlockSpec((None, qb, nq), lambda b, i: (b, i, 0)),
        out_shape=jax.ShapeDtypeStruct((B, T, nq), BF16),
        scratch_shapes=[pltpu.VMEM((N_KV_GROUPS, n_slc, qb), F32)],
        compiler_params=_cparams("arbitrary", "arbitrary"),
        name="nsa_attention",
    )(q, qr, kc, vct, ks, vst, kw, vwt, gates, jnp.asarray(mt, BF16))


def _proj_ln_kernel(a_ref, w_ref, x_ref, lng_ref, lnb_ref, o_ref):
    mix = jnp.dot(a_ref[...], w_ref[...], preferred_element_type=F32)
    o_ref[...] = _layer_norm(DEEPNORM_ALPHA * x_ref[...] + mix, lng_ref[...], lnb_ref[...])


def proj_ln(a, w, x, ln_g, ln_b, tq=512):
    N, D = x.shape
    K = a.shape[1]
    row = lambda v: v.reshape(1, -1).astype(F32)
    const = lambda shape: pl.BlockSpec(shape, lambda i: (0,) * len(shape))
    return pl.pallas_call(
        _proj_ln_kernel,
        grid=(N // tq,),
        in_specs=[pl.BlockSpec((tq, K), lambda i: (i, 0)), const((K, D)), pl.BlockSpec((tq, D), lambda i: (i, 0)),
                  const((1, D)), const((1, D))],
        out_specs=pl.BlockSpec((tq, D), lambda i: (i, 0)),
        out_shape=jax.ShapeDtypeStruct((N, D), F32),
        compiler_params=_cparams("arbitrary"),
        name="proj_ln",
    )(a, w.astype(BF16), x, row(ln_g), row(ln_b))


def nsa_layer(x, w_in, cmp_pos_k, cmp_w1_k, cmp_w2_k, cmp_pos_v, cmp_w1_v, cmp_w2_v, w_out, ln_g, ln_b):
    B, T, D = x.shape
    q, qr, kc, vc, ks, kw, vst, vwt, gates = nsa_project(x, w_in)
    kc_cmp = nsa_compress(kc, cmp_pos_k, cmp_w1_k, cmp_w2_k)
    vc_cmp = nsa_compress(vc, cmp_pos_v, cmp_w1_v, cmp_w2_v)
    o = nsa_attention(q, qr, kc_cmp, jnp.swapaxes(vc_cmp, 1, 2), ks, vst, kw, vwt, gates)
    return proj_ln(o.reshape(B * T, -1), w_out, x.reshape(B * T, D), ln_g, ln_b)


def kernel(x, ab_w_in, ab_pool_mix, ab_pool_scale, ab_conv_w, ab_conv_b, ab_conv_ln_g, ab_conv_ln_b, ab_w_out, nsa_w_in, nsa_cmp_pos_k, nsa_cmp_w1_k, nsa_cmp_w2_k, nsa_cmp_pos_v, nsa_cmp_w1_v, nsa_cmp_w2_v, nsa_w_out, ln_mix_g, ln_mix_b, ln_ffn_g, ln_ffn_b, moe_w_router, moe_b_router, moe_w_gate, moe_w_up, moe_w_down, moe_ws_gate, moe_ws_up, moe_ws_down):
    B, T, D = x.shape
    moe = lambda h, i: moe_layer(h, i, moe_w_router[i], moe_b_router[i], moe_w_gate, moe_w_up, moe_w_down,
                                 moe_ws_gate[i], moe_ws_up[i], moe_ws_down[i], ln_ffn_g[i], ln_ffn_b[i])
    h = ab_mixer(x, ab_w_in[0], ab_pool_mix[0], ab_pool_scale[0], ab_conv_w[0], ab_conv_b[0], ab_conv_ln_g[0],
                 ab_conv_ln_b[0], ab_w_out[0], ln_mix_g[0], ln_mix_b[0])
    h = moe(h.reshape(B * T, D), 0)
    h = nsa_layer(h.reshape(B, T, D), nsa_w_in[0], nsa_cmp_pos_k[0], nsa_cmp_w1_k[0], nsa_cmp_w2_k[0],
                  nsa_cmp_pos_v[0], nsa_cmp_w1_v[0], nsa_cmp_w2_v[0], nsa_w_out[0], ln_mix_g[1], ln_mix_b[1])
    h = moe(h, 1)
    return h.reshape(B, T, D)
```

```python
import functools

import jax
import jax.numpy as jnp
import numpy as np
from jax import lax
from jax.experimental import pallas as pl
from jax.experimental.pallas import tpu as pltpu

F32 = jnp.float32
BF16 = jnp.bfloat16
I32 = jnp.int32

D_MODEL = 1024
DEPTH = 2
DEEPNORM_ALPHA = (2.0 * DEPTH) ** 0.25
LN_EPS = 1e-5

POOL_WIDTH = 512
POOL_WINDOWS = (2, 4, 8, 16)
POOL_GROUP = 128
CONV_CH = 512
CONV_TAPS = 31
AB_IN = POOL_WIDTH + 2 * CONV_CH
HALO = 32
CONV_ROWS = 32

N_HEADS = 16
HEAD_DIM = 64
N_KV_GROUPS = 4
HEADS_PER_GROUP = 4
KV_WIDTH = 256
CMP_BLOCK = 32
CMP_STRIDE = 16
CMP_HIDDEN = 128
SLC_BLOCK = 64
N_SELECT = 16
WINDOW = 512
N_BRANCH = 3
ROPE_THETA = 500000.0
ROPE_DIM = 16
SLC_OVERLAP_W = (1.0, 2.0, 2.0, 2.0, 1.0)

N_EXPERTS = 256
TOP_K = 8
N_EXPERT_GROUPS = 8
TOPK_GROUPS = 4
EXPERT_FF = 256
SHARED_FF = 256
ROUTED_SCALE = 2.5
EXPERT_ROWS = 256
MOE_TQ = 256
SLOT_ROWS = 8
CHUNK_ROWS = 16
EXPERT_BLOCK = 32
TN_DIMS = (((0,), (0,)), ((), ()))

VMEM_LIMIT = 56 * 1024 * 1024
NEG_INF = float("-inf")


def _cparams(*sem):
    return pltpu.CompilerParams(dimension_semantics=sem, vmem_limit_bytes=VMEM_LIMIT)


def _layer_norm(y, g, b):
    mu = jnp.mean(y, axis=-1, keepdims=True)
    d = y - mu
    var = jnp.mean(d * d, axis=-1, keepdims=True)
    return d * lax.rsqrt(var + LN_EPS) * g + b


def _silu(v):
    return v * jax.nn.sigmoid(v)


def _ab_mixer_kernel(x_ref, win_ref, pmix_ref, pscale_ref, cw_ref, cb_ref, clg_ref, clb_ref,
                     wout_ref, lng_ref, lnb_ref, o_ref, eu_ref, ev_ref, cat_ref, *, tq):
    ti = pl.program_id(1)

    @pl.when(ti == 0)
    def _():
        eu_ref[0:HALO, :] = jnp.zeros((HALO, POOL_WIDTH), F32)
        ev_ref[0:HALO, :] = jnp.zeros((HALO, CONV_CH), F32)

    @pl.when(ti > 0)
    def _():
        eu_ref[0:HALO, :] = eu_ref[tq:tq + HALO, :]
        ev_ref[0:HALO, :] = ev_ref[tq:tq + HALO, :]

    x = x_ref[...]
    h = jnp.dot(x.astype(BF16), win_ref[...], preferred_element_type=F32)
    u = h[:, :POOL_WIDTH]
    eu_ref[HALO:HALO + tq, :] = u
    ev_ref[HALO:HALO + tq, :] = h[:, POOL_WIDTH:POOL_WIDTH + CONV_CH] * jax.nn.sigmoid(
        h[:, POOL_WIDTH + CONV_CH:])

    pos1 = ti * tq + lax.broadcasted_iota(I32, (tq, 1), 0) + 1
    for gi, w in enumerate(POOL_WINDOWS):
        lo, hi = gi * POOL_GROUP, (gi + 1) * POOL_GROUP
        ug = u[:, lo:hi]
        s = ug
        for j in range(1, w):
            s = s + eu_ref[HALO - j:HALO - j + tq, lo:hi]
        cnt = jnp.minimum(pos1, w).astype(F32)
        p = s / cnt - ug
        a = jnp.dot(p.astype(BF16), pmix_ref[gi], preferred_element_type=F32)
        cat_ref[:, lo:hi] = (a * pscale_ref[:, lo:hi]).astype(BF16)

    for r in range(tq // CONV_ROWS):
        base = r * CONV_ROWS + HALO - (CONV_TAPS - 1)
        acc = jnp.broadcast_to(cb_ref[...], (CONV_ROWS, CONV_CH))
        for k in range(CONV_TAPS):
            acc = acc + cw_ref[k:k + 1, :] * ev_ref[base + k:base + k + CONV_ROWS, :]
        c = _silu(_layer_norm(acc, clg_ref[...], clb_ref[...]))
        cat_ref[r * CONV_ROWS:(r + 1) * CONV_ROWS, POOL_WIDTH:] = c.astype(BF16)

    mix = jnp.dot(cat_ref[...], wout_ref[...], preferred_element_type=F32)
    o_ref[...] = _layer_norm(DEEPNORM_ALPHA * x + mix, lng_ref[...], lnb_ref[...])


def ab_mixer(x, w_in, pool_mix, pool_scale, conv_w, conv_b, conv_ln_g, conv_ln_b, w_out, ln_g, ln_b,
             tq=256):
    B, T, D = x.shape
    cw = jnp.pad(conv_w.reshape(CONV_TAPS, CONV_CH), ((0, 32 - CONV_TAPS), (0, 0)))
    row = lambda v: v.reshape(1, -1).astype(F32)
    const = lambda shape: pl.BlockSpec(shape, lambda b, t: (0,) * len(shape))
    return pl.pallas_call(
        functools.partial(_ab_mixer_kernel, tq=tq),
        grid=(B, T // tq),
        in_specs=[
            pl.BlockSpec((None, tq, D), lambda b, t: (b, t, 0)),
            const((D, AB_IN)), const((4, POOL_GROUP, POOL_GROUP)), const((1, POOL_WIDTH)),
            const((32, CONV_CH)), const((1, CONV_CH)), const((1, CONV_CH)), const((1, CONV_CH)),
            const((D, D)), const((1, D)), const((1, D)),
        ],
        out_specs=pl.BlockSpec((None, tq, D), lambda b, t: (b, t, 0)),
        out_shape=jax.ShapeDtypeStruct((B, T, D), F32),
        scratch_shapes=[pltpu.VMEM((HALO + tq, POOL_WIDTH), F32),
                        pltpu.VMEM((HALO + tq, CONV_CH), F32),
                        pltpu.VMEM((tq, D), BF16)],
        compiler_params=_cparams("arbitrary", "arbitrary"),
        name="ab_mixer",
    )(x, w_in.astype(BF16), pool_mix.astype(BF16), row(pool_scale), cw, row(conv_b),
      row(conv_ln_g), row(conv_ln_b), w_out.astype(BF16), row(ln_g), row(ln_b))


def _first_index(hit_src, m, iota, size):
    return jnp.min(jnp.where(hit_src == m, iota, size), axis=0, keepdims=True)


def _router_kernel(x_ref, wh_ref, wl_ref, b_ref, tri_ref, idx_ref, wt_ref, lrank_ref, cnt_ref, *, tq):
    x = x_ref[...]
    xh = x.astype(BF16)
    xl = (x - xh.astype(F32)).astype(BF16)
    nt = (((1,), (1,)), ((), ()))
    wh, wl = wh_ref[...], wl_ref[...]
    logits = (lax.dot_general(wh, xh, nt, preferred_element_type=F32)
              + lax.dot_general(wh, xl, nt, preferred_element_type=F32)
              + lax.dot_general(wl, xh, nt, preferred_element_type=F32))
    scores = jax.nn.sigmoid(logits)
    biased = scores + b_ref[...]

    per_group = N_EXPERTS // N_EXPERT_GROUPS
    io_g = lax.broadcasted_iota(I32, (per_group, tq), 0)
    gs_rows = []
    for g in range(N_EXPERT_GROUPS):
        sg = biased[g * per_group:(g + 1) * per_group, :]
        m1 = jnp.max(sg, axis=0, keepdims=True)
        i1 = _first_index(sg, m1, io_g, per_group)
        m2 = jnp.max(jnp.where(io_g == i1, NEG_INF, sg), axis=0, keepdims=True)
        gs_rows.append(m1 + m2)
    gs = jnp.concatenate(gs_rows, axis=0)

    io_ng = lax.broadcasted_iota(I32, (N_EXPERT_GROUPS, tq), 0)
    grp_sel = jnp.zeros((N_EXPERT_GROUPS, tq), jnp.bool_)
    for _ in range(TOPK_GROUPS):
        m = jnp.max(gs, axis=0, keepdims=True)
        hit = io_ng == _first_index(gs, m, io_ng, N_EXPERT_GROUPS)
        grp_sel = grp_sel | hit
        gs = jnp.where(hit, NEG_INF, gs)
    exp_mask = jnp.concatenate(
        [jnp.broadcast_to(grp_sel[g:g + 1, :], (per_group, tq)) for g in range(N_EXPERT_GROUPS)], axis=0)

    masked = jnp.where(exp_mask, biased, NEG_INF)
    io_e = lax.broadcasted_iota(I32, (N_EXPERTS, tq), 0)
    idx_rows, w_rows = [], []
    onehot = jnp.zeros((N_EXPERTS, tq), F32)
    for _ in range(TOP_K):
        m = jnp.max(masked, axis=0, keepdims=True)
        first = _first_index(masked, m, io_e, N_EXPERTS)
        hit = io_e == first
        idx_rows.append(first)
        w_rows.append(jnp.sum(jnp.where(hit, scores, 0.0), axis=0, keepdims=True))
        onehot = onehot + hit.astype(F32)
        masked = jnp.where(hit, NEG_INF, masked)
    w = jnp.concatenate(w_rows, axis=0)
    wt_ref[...] = w / jnp.sum(w, axis=0, keepdims=True) * ROUTED_SCALE
    idx_ref[...] = jnp.concatenate(idx_rows, axis=0)

    oh = onehot.astype(BF16)
    before = jnp.dot(oh, tri_ref[...], preferred_element_type=F32)
    lrank_ref[...] = jnp.concatenate(
        [jnp.sum(jnp.where(io_e == r, before, 0.0), axis=0, keepdims=True) for r in idx_rows],
        axis=0).astype(I32)
    cnt_ref[...] = lax.dot_general(jnp.ones((8, tq), BF16), oh, nt, preferred_element_type=F32).astype(I32)


def moe_router(x, w_router, b_router, tq):
    N, D = x.shape
    wt = w_router.T
    wh = wt.astype(BF16)
    wl = (wt - wh.astype(F32)).astype(BF16)
    tri = (np.arange(tq)[:, None] < np.arange(tq)[None, :]).astype(np.float32)
    const = lambda shape: pl.BlockSpec(shape, lambda i: (0,) * len(shape))
    tok = lambda rows: pl.BlockSpec((rows, tq), lambda i: (0, i))
    return pl.pallas_call(
        functools.partial(_router_kernel, tq=tq),
        grid=(N // tq,),
        in_specs=[pl.BlockSpec((tq, D), lambda i: (i, 0)), const((N_EXPERTS, D)), const((N_EXPERTS, D)),
                  const((N_EXPERTS, 1)), const((tq, tq))],
        out_specs=[tok(TOP_K), tok(TOP_K), tok(TOP_K), pl.BlockSpec((None, 8, N_EXPERTS), lambda i: (i, 0, 0))],
        out_shape=[jax.ShapeDtypeStruct((TOP_K, N), I32), jax.ShapeDtypeStruct((TOP_K, N), F32),
                   jax.ShapeDtypeStruct((TOP_K, N), I32), jax.ShapeDtypeStruct((N // tq, 8, N_EXPERTS), I32)],
        compiler_params=_cparams("arbitrary"),
        name="moe_router",
    )(x, wh, wl, b_router.reshape(N_EXPERTS, 1).astype(F32), jnp.asarray(tri, BF16))


def _pack_pairs(v):
    half = v.shape[1] // 2
    hi = lax.bitcast_convert_type(v[:, :half], I32)
    lo = lax.bitcast_convert_type(v[:, half:], I32)
    return (hi & jnp.int32(-65536)) | lax.shift_right_logical(lo, jnp.int32(16))


def _unpack_pairs(p):
    hi = lax.bitcast_convert_type(p & jnp.int32(-65536), F32)
    lo = lax.bitcast_convert_type(lax.shift_left(p, jnp.int32(16)), F32)
    return jnp.concatenate([hi, lo], axis=1).astype(BF16)


def _round_bf16(v):
    return v.astype(BF16).astype(F32)


def _slot_copies(cnt_ref, off_ref, hbm, buf, sem, to_hbm):
    def issue(e, carry):
        n_small, n_big = carry
        c = cnt_ref[0, 0, e]
        off = pl.multiple_of(off_ref[0, 0, e], SLOT_ROWS)
        small = (c > 0) & (c <= SLOT_ROWS)
        big = c > SLOT_ROWS

        base = pl.multiple_of(e * CHUNK_ROWS, CHUNK_ROWS)

        @pl.when(small)
        def _():
            pair = (buf.at[pl.ds(base, SLOT_ROWS), :], hbm.at[pl.ds(off, SLOT_ROWS), :])
            pltpu.make_async_copy(*(pair if to_hbm else pair[::-1]), sem).start()

        @pl.when(big)
        def _():
            pair = (buf.at[pl.ds(base, CHUNK_ROWS), :], hbm.at[pl.ds(off, CHUNK_ROWS), :])
            pltpu.make_async_copy(*(pair if to_hbm else pair[::-1]), sem).start()
        return n_small + small.astype(I32), n_big + big.astype(I32)
    return lax.fori_loop(0, N_EXPERTS, issue, (0, 0))


def _slot_waits(n_small, n_big, hbm, buf, sem, to_hbm):
    def drain(rows):
        def body(_, c):
            pair = (buf.at[pl.ds(0, rows), :], hbm.at[pl.ds(0, rows), :])
            pltpu.make_async_copy(*(pair if to_hbm else pair[::-1]), sem).wait()
            return c
        return body
    lax.fori_loop(0, n_small, drain(SLOT_ROWS), 0)
    lax.fori_loop(0, n_big, drain(CHUNK_ROWS), 0)


def _slot_rows(lrk_at, fill, tq):
    io_r = lax.broadcasted_iota(I32, (CHUNK_ROWS, tq), 0)
    blocks = []
    for eb in range(N_EXPERTS // EXPERT_BLOCK):
        rows = []
        for e in range(eb * EXPERT_BLOCK, (eb + 1) * EXPERT_BLOCK):
            hit = jnp.broadcast_to(lrk_at[e:e + 1, :], (CHUNK_ROWS, tq)) == io_r
            rows.append(fill(e, hit))
        blocks.append(jnp.concatenate(rows, axis=0))
    return blocks


def _dispatch_kernel(cnt_ref, off_ref, x_ref, idx_ref, lrk_ref, xs_hbm, xbuf, obuf, sem, *, tq):
    xb = x_ref[...].astype(BF16)
    io_e = lax.broadcasted_iota(I32, (N_EXPERTS, tq), 0)
    lrk_at = jnp.full((N_EXPERTS, tq), -1, I32)
    for k in range(TOP_K):
        lrk_at = jnp.where(io_e == idx_ref[k:k + 1, :], lrk_ref[k:k + 1, :], lrk_at)
    rows_per_block = EXPERT_BLOCK * CHUNK_ROWS
    for eb, pick in enumerate(_slot_rows(lrk_at, lambda e, hit: hit.astype(BF16), tq)):
        xbuf[eb * rows_per_block:(eb + 1) * rows_per_block, :] = _pack_pairs(
            jnp.dot(pick, xb, preferred_element_type=F32))
    n_small, n_big = _slot_copies(cnt_ref, off_ref, xs_hbm, xbuf, sem, True)
    _slot_waits(n_small, n_big, xs_hbm, xbuf, sem, True)

    io_r = lax.broadcasted_iota(I32, (SLOT_ROWS, tq), 0)

    def overflow(e, c):
        n_units = (cnt_ref[0, 0, e] + SLOT_ROWS - 1) // SLOT_ROWS
        off = pl.multiple_of(off_ref[0, 0, e], SLOT_ROWS)

        def extra(j, c2):
            rank_e = jnp.full((1, tq), -1, I32)
            for k in range(TOP_K):
                rank_e = jnp.where(idx_ref[k:k + 1, :] == e, lrk_ref[k:k + 1, :], rank_e)
            pick = (io_r + j * SLOT_ROWS == rank_e).astype(BF16)
            obuf[...] = _pack_pairs(jnp.dot(pick, xb, preferred_element_type=F32))
            cp = pltpu.make_async_copy(obuf, xs_hbm.at[pl.ds(off + j * SLOT_ROWS, SLOT_ROWS), :], sem)
            cp.start()
            cp.wait()
            return c2
        return lax.fori_loop(CHUNK_ROWS // SLOT_ROWS, n_units, extra, c)
    lax.fori_loop(0, N_EXPERTS, overflow, 0)


def moe_dispatch(x, idx, lrank, tile_cnt, tile_off, n_rows, tq):
    N, D = x.shape
    n_tiles = N // tq
    smem = lambda: pl.BlockSpec((1, 1, N_EXPERTS), lambda i: (i, 0, 0), memory_space=pltpu.SMEM)
    return pl.pallas_call(
        functools.partial(_dispatch_kernel, tq=tq),
        grid=(n_tiles,),
        in_specs=[smem(), smem(), pl.BlockSpec((tq, D), lambda i: (i, 0)),
                  pl.BlockSpec((TOP_K, tq), lambda i: (0, i)), pl.BlockSpec((TOP_K, tq), lambda i: (0, i))],
        out_specs=pl.BlockSpec(memory_space=pl.ANY),
        out_shape=jax.ShapeDtypeStruct((n_rows, D // 2), I32),
        scratch_shapes=[pltpu.VMEM((N_EXPERTS * CHUNK_ROWS, D // 2), I32), pltpu.VMEM((SLOT_ROWS, D // 2), I32),
                        pltpu.SemaphoreType.DMA(())],
        compiler_params=_cparams("arbitrary"),
        name="moe_dispatch",
    )(tile_cnt.reshape(n_tiles, 1, N_EXPERTS), tile_off.reshape(n_tiles, 1, N_EXPERTS), x, idx, lrank)


def _expert_kernel(row0_ref, nblk_ref, used_ref, x_hbm, wg_ref, wu_ref, wd_ref, y_hbm, wgb, wub, wdb, xin, yout,
                   in_sem, out_sem):
    e = pl.program_id(0)
    m = EXPERT_ROWS
    nb = nblk_ref[e]
    row0 = row0_ref[e]
    rows_of = lambda j: pl.ds(pl.multiple_of(row0 + j * m, m), m)
    in_copy = lambda j, slot: pltpu.make_async_copy(x_hbm.at[rows_of(j), :], xin.at[slot], in_sem.at[slot])
    out_copy = lambda j, slot: pltpu.make_async_copy(yout.at[slot], y_hbm.at[rows_of(j), :], out_sem.at[slot])

    @pl.when(nb > 0)
    def _():
        in_copy(0, 0).start()
        wgb[...] = wg_ref[...].astype(BF16)
        wub[...] = wu_ref[...].astype(BF16)
        wdb[...] = wd_ref[...].astype(BF16)

        def block(j, c):
            slot = j % 2
            in_copy(j, slot).wait()

            @pl.when(j + 1 < nb)
            def _():
                in_copy(j + 1, 1 - slot).start()

            @pl.when(j >= 2)
            def _():
                out_copy(j - 2, slot).wait()

            rows = lax.broadcasted_iota(I32, (m, xin.shape[2]), 0)
            xb = _unpack_pairs(jnp.where(rows < used_ref[e] - j * m, xin[slot], 0))
            g = jnp.dot(xb, wgb[...], preferred_element_type=F32)
            u = jnp.dot(xb, wub[...], preferred_element_type=F32)
            hid = (_silu(g) * u).astype(BF16)
            yout[slot] = _pack_pairs(_round_bf16(jnp.dot(hid, wdb[...], preferred_element_type=F32)))
            out_copy(j, slot).start()
            return c
        lax.fori_loop(0, nb, block, 0)

        @pl.when(nb >= 2)
        def _():
            out_copy(nb - 2, nb % 2).wait()
        out_copy(nb - 1, (nb - 1) % 2).wait()


def moe_experts(xs, row0, nblk, used, w_gate, w_up, w_down, layer):
    m = EXPERT_ROWS
    D = xs.shape[1] * 2
    wspec = lambda r, c: pl.BlockSpec((None, None, r, c), lambda e, *_: (layer, e, 0, 0))
    grid_spec = pltpu.PrefetchScalarGridSpec(
        num_scalar_prefetch=3,
        grid=(N_EXPERTS,),
        in_specs=[pl.BlockSpec(memory_space=pl.ANY), wspec(D, EXPERT_FF), wspec(D, EXPERT_FF), wspec(EXPERT_FF, D)],
        out_specs=pl.BlockSpec(memory_space=pl.ANY),
        scratch_shapes=[pltpu.VMEM((D, EXPERT_FF), BF16), pltpu.VMEM((D, EXPERT_FF), BF16),
                        pltpu.VMEM((EXPERT_FF, D), BF16), pltpu.VMEM((2, m, D // 2), I32),
                        pltpu.VMEM((2, m, D // 2), I32), pltpu.SemaphoreType.DMA((2,)),
                        pltpu.SemaphoreType.DMA((2,))],
    )
    return pl.pallas_call(
        _expert_kernel,
        grid_spec=grid_spec,
        out_shape=jax.ShapeDtypeStruct(xs.shape, I32),
        compiler_params=_cparams("arbitrary"),
        name="moe_experts",
    )(row0, nblk, used, xs, w_gate, w_up, w_down)


def _combine_kernel(cnt_ref, off_ref, y_hbm, idx_ref, lrk_ref, w_ref, x_ref, wsg_ref, wsu_ref, wsd_ref,
                    lng_ref, lnb_ref, o_ref, ybuf, obuf, acc_ref, sem, *, tq):
    @pl.when(pl.program_id(0) == 0)
    def _():
        ybuf[...] = jnp.zeros_like(ybuf)

    n_small, n_big = _slot_copies(cnt_ref, off_ref, y_hbm, ybuf, sem, False)

    x = x_ref[...]
    xb = x.astype(BF16)
    g = jnp.dot(xb, wsg_ref[...], preferred_element_type=F32)
    u = jnp.dot(xb, wsu_ref[...], preferred_element_type=F32)
    acc_ref[...] = jnp.dot((_silu(g) * u).astype(BF16), wsd_ref[...], preferred_element_type=F32)

    io_e = lax.broadcasted_iota(I32, (N_EXPERTS, tq), 0)
    w_at = jnp.zeros((N_EXPERTS, tq), F32)
    lrk_at = jnp.full((N_EXPERTS, tq), -1, I32)
    for k in range(TOP_K):
        hit = io_e == idx_ref[k:k + 1, :]
        w_at = jnp.where(hit, w_ref[k:k + 1, :], w_at)
        lrk_at = jnp.where(hit, lrk_ref[k:k + 1, :], lrk_at)
    fill = lambda e, hit: jnp.where(hit, jnp.broadcast_to(w_at[e:e + 1, :], hit.shape), 0.0).astype(BF16)
    weights = _slot_rows(lrk_at, fill, tq)

    _slot_waits(n_small, n_big, y_hbm, ybuf, sem, False)

    rows_per_block = EXPERT_BLOCK * CHUNK_ROWS
    for eb, wct in enumerate(weights):
        yb = _unpack_pairs(ybuf[eb * rows_per_block:(eb + 1) * rows_per_block, :])
        acc_ref[...] += lax.dot_general(wct, yb, TN_DIMS, preferred_element_type=F32)

    io_r = lax.broadcasted_iota(I32, (SLOT_ROWS, tq), 0)

    def overflow(e, c):
        n_units = (cnt_ref[0, 0, e] + SLOT_ROWS - 1) // SLOT_ROWS
        off = pl.multiple_of(off_ref[0, 0, e], SLOT_ROWS)

        def extra(j, c2):
            cp = pltpu.make_async_copy(y_hbm.at[pl.ds(off + j * SLOT_ROWS, SLOT_ROWS), :], obuf, sem)
            cp.start()
            w_e = jnp.zeros((1, tq), F32)
            rank_e = jnp.full((1, tq), -1, I32)
            for k in range(TOP_K):
                hit = idx_ref[k:k + 1, :] == e
                w_e = jnp.where(hit, w_ref[k:k + 1, :], w_e)
                rank_e = jnp.where(hit, lrk_ref[k:k + 1, :], rank_e)
            wct = jnp.where(io_r + j * SLOT_ROWS == rank_e, w_e, 0.0).astype(BF16)
            cp.wait()
            acc_ref[...] += lax.dot_general(wct, _unpack_pairs(obuf[...]), TN_DIMS, preferred_element_type=F32)
            return c2
        return lax.fori_loop(CHUNK_ROWS // SLOT_ROWS, n_units, extra, c)
    lax.fori_loop(0, N_EXPERTS, overflow, 0)

    o_ref[...] = _layer_norm(DEEPNORM_ALPHA * x + acc_ref[...], lng_ref[...], lnb_ref[...])


def moe_combine(x, ys, idx, lrank, wts, tile_cnt, tile_off, ws_gate, ws_up, ws_down, ln_g, ln_b, tq):
    N, D = x.shape
    n_tiles = N // tq
    row = lambda v: v.reshape(1, -1).astype(F32)
    const = lambda shape: pl.BlockSpec(shape, lambda i: (0,) * len(shape))
    smem = lambda: pl.BlockSpec((1, 1, N_EXPERTS), lambda i: (i, 0, 0), memory_space=pltpu.SMEM)
    tokl = lambda: pl.BlockSpec((TOP_K, tq), lambda i: (0, i))
    return pl.pallas_call(
        functools.partial(_combine_kernel, tq=tq),
        grid=(n_tiles,),
        in_specs=[smem(), smem(), pl.BlockSpec(memory_space=pl.ANY), tokl(), tokl(), tokl(),
                  pl.BlockSpec((tq, D), lambda i: (i, 0)),
                  const((D, SHARED_FF)), const((D, SHARED_FF)), const((SHARED_FF, D)), const((1, D)), const((1, D))],
        out_specs=pl.BlockSpec((tq, D), lambda i: (i, 0)),
        out_shape=jax.ShapeDtypeStruct((N, D), F32),
        scratch_shapes=[pltpu.VMEM((N_EXPERTS * CHUNK_ROWS, D // 2), I32), pltpu.VMEM((SLOT_ROWS, D // 2), I32),
                        pltpu.VMEM((tq, D), F32), pltpu.SemaphoreType.DMA(())],
        compiler_params=_cparams("arbitrary"),
        name="moe_combine",
    )(tile_cnt.reshape(n_tiles, 1, N_EXPERTS), tile_off.reshape(n_tiles, 1, N_EXPERTS), ys, idx, lrank,
      wts, x, ws_gate.astype(BF16), ws_up.astype(BF16), ws_down.astype(BF16), row(ln_g), row(ln_b))


def moe_layer(x, layer, w_router, b_router, w_gate, w_up, w_down, ws_gate, ws_up, ws_down, ln_g, ln_b):
    N, D = x.shape
    m = EXPERT_ROWS
    tq = MOE_TQ
    idx, wts, lrank, tile_cnt = moe_router(x, w_router, b_router, tq)
    tile_cnt = tile_cnt[:, 0, :]
    slots = (tile_cnt + SLOT_ROWS - 1) // SLOT_ROWS * SLOT_ROWS
    used = jnp.sum(slots, axis=0)
    padded = (used + m - 1) // m * m
    pend = jnp.cumsum(padded)
    pstart = pend - padded
    tile_off = pstart[None, :] + jnp.cumsum(slots, axis=0) - slots
    n_assign = N * TOP_K
    max_rows = n_assign + (SLOT_ROWS - 1) * min(n_assign, (N // tq) * N_EXPERTS) + N_EXPERTS * (m - 1)
    xs = moe_dispatch(x, idx, lrank, tile_cnt, tile_off, -(-max_rows // m) * m, tq)
    ys = moe_experts(xs, pstart.astype(I32), (padded // m).astype(I32), used.astype(I32), w_gate, w_up, w_down, layer)
    return moe_combine(x, ys, idx, lrank, wts, tile_cnt, tile_off, ws_gate, ws_up, ws_down, ln_g, ln_b, tq)


Q_TILE = 128
SEL_KEYS = 512
WIN_TILE = 128
WIN_KEYS = WINDOW + Q_TILE
PROJ_TQ = SEL_KEYS
KS_WIDTH = N_KV_GROUPS * 128
VS_ROWS = HEAD_DIM + 16
NSA_TOK_COLS = N_HEADS * HEAD_DIM + 3 * KV_WIDTH + KS_WIDTH
NSA_FEAT_ROWS = 2 * KV_WIDTH + N_HEADS * N_BRANCH
NT_DIMS = (((1,), (1,)), ((), ()))
LOG2E = 1.4426950408889634
MASK_BIAS = -1e30


def _rotary(v, cos, sin, lo_mask):
    parts = []
    for c in range(v.shape[1] // 128):
        blk = v[:, c * 128:(c + 1) * 128]
        nxt = pltpu.roll(blk, 128 - ROPE_DIM // 2, axis=1)
        prv = pltpu.roll(blk, ROPE_DIM // 2, axis=1)
        parts.append(blk * cos + jnp.where(lo_mask, nxt, prv) * sin)
    return jnp.concatenate(parts, axis=1)


def _nsa_proj_kernel(x_ref, wtok_ref, wfeat_ref, cos_ref, sin_ref, cosk_ref, sink_ref, q_ref, qr_ref, kc_ref,
                     vc_ref, ks_ref, kw_ref, vs_ref, vw_ref, g_ref, *, tq):
    xb = x_ref[...].astype(BF16)
    h = jnp.dot(xb, wtok_ref[...], preferred_element_type=F32)
    cos, sin = cos_ref[...], sin_ref[...]
    lane = lax.broadcasted_iota(I32, (tq, 128), 1)
    lo_mask = (lane % HEAD_DIM) < ROPE_DIM // 2
    nq = N_HEADS * HEAD_DIM
    q = h[:, :nq] * (HEAD_DIM ** -0.5 * LOG2E)
    q_ref[...] = q.astype(BF16)
    qr_ref[...] = _rotary(q, cos, sin, lo_mask).astype(BF16)
    c0 = nq
    kc_ref[...] = h[:, c0:c0 + KV_WIDTH].astype(BF16)
    vc_ref[...] = h[:, c0 + KV_WIDTH:c0 + 2 * KV_WIDTH].astype(BF16)
    c0 += 2 * KV_WIDTH
    ks = _rotary(h[:, c0:c0 + N_KV_GROUPS * 128], cosk_ref[...], sink_ref[...], lo_mask)
    blk = lax.broadcasted_iota(I32, (tq, 128), 0) % SEL_KEYS // SLC_BLOCK
    onehot = (lane - HEAD_DIM == blk).astype(F32)
    ks_ref[...] = (ks + jnp.concatenate([onehot] * N_KV_GROUPS, axis=1)).astype(BF16)
    c0 += N_KV_GROUPS * 128
    kw_ref[...] = _rotary(h[:, c0:c0 + KV_WIDTH], cos, sin, lo_mask).astype(BF16)
    hf = lax.dot_general(wfeat_ref[...], xb, NT_DIMS, preferred_element_type=F32)
    ones_row = (lax.broadcasted_iota(I32, (VS_ROWS - HEAD_DIM, tq), 0) == 0).astype(BF16)
    for g in range(N_KV_GROUPS):
        vs_ref[g, 0:HEAD_DIM, :] = hf[g * HEAD_DIM:(g + 1) * HEAD_DIM].astype(BF16)
        vs_ref[g, HEAD_DIM:VS_ROWS, :] = ones_row
    for j in range(tq // WIN_TILE):
        vw_ref[j] = hf[KV_WIDTH:2 * KV_WIDTH, j * WIN_TILE:(j + 1) * WIN_TILE].astype(BF16)
    g_ref[...] = jax.nn.sigmoid(hf[2 * KV_WIDTH:])


def nsa_project(x, w_in, tq=PROJ_TQ):
    B, T, D = x.shape
    nq = N_HEADS * HEAD_DIM
    q, kc, vc, ks, vs, kw, vw, g = jnp.split(w_in, [nq + i * KV_WIDTH for i in range(7)], axis=1)
    ks_wide = jnp.pad(ks.reshape(D, N_KV_GROUPS, HEAD_DIM), ((0, 0), (0, 0), (0, 128 - HEAD_DIM)))
    wtok = jnp.concatenate([q, kc, vc, ks_wide.reshape(D, KS_WIDTH), kw], axis=1).astype(BF16)
    wfeat = jnp.concatenate([vs, vw, g], axis=1).T.astype(BF16)
    inv_freq = jnp.power(jnp.float32(ROPE_THETA), -jnp.arange(0, ROPE_DIM, 2, dtype=F32) / ROPE_DIM)
    ang = jnp.arange(T).astype(F32)[:, None] * inv_freq[None, :]
    ones = jnp.ones((T, HEAD_DIM - ROPE_DIM), F32)
    cos_h = jnp.concatenate([jnp.cos(ang), jnp.cos(ang), ones], axis=1)
    sin_h = jnp.concatenate([-jnp.sin(ang), jnp.sin(ang), 0.0 * ones], axis=1)
    ident = jnp.ones((T, 128 - HEAD_DIM), F32)
    tables = [jnp.concatenate([cos_h, cos_h], axis=1), jnp.concatenate([sin_h, sin_h], axis=1),
              jnp.concatenate([cos_h, ident], axis=1), jnp.concatenate([sin_h, 0.0 * ident], axis=1)]
    nt = T // tq
    tokm = lambda w: pl.BlockSpec((None, tq, w), lambda b, t: (b, t, 0))
    const = lambda shape: pl.BlockSpec(shape, lambda b, t: (0,) * len(shape))
    table = lambda: pl.BlockSpec((tq, 128), lambda b, t: (t, 0))
    bf = lambda *shape: jax.ShapeDtypeStruct(shape, BF16)
    return pl.pallas_call(
        functools.partial(_nsa_proj_kernel, tq=tq),
        grid=(B, nt),
        in_specs=[tokm(D), const((D, NSA_TOK_COLS)), const((NSA_FEAT_ROWS, D)), table(), table(), table(), table()],
        out_specs=[tokm(nq), tokm(nq), tokm(KV_WIDTH), tokm(KV_WIDTH), tokm(KS_WIDTH), tokm(KV_WIDTH),
                   pl.BlockSpec((None, None, N_KV_GROUPS, VS_ROWS, tq), lambda b, t: (b, t, 0, 0, 0)),
                   pl.BlockSpec((None, tq // WIN_TILE, KV_WIDTH, WIN_TILE), lambda b, t: (b, t, 0, 0)),
                   pl.BlockSpec((None, N_HEADS * N_BRANCH, tq), lambda b, t: (b, 0, t))],
        out_shape=[bf(B, T, nq), bf(B, T, nq), bf(B, T, KV_WIDTH), bf(B, T, KV_WIDTH), bf(B, T, KS_WIDTH),
                   bf(B, T, KV_WIDTH), bf(B, T // tq, N_KV_GROUPS, VS_ROWS, tq),
                   bf(B, T // WIN_TILE, KV_WIDTH, WIN_TILE),
                   jax.ShapeDtypeStruct((B, N_HEADS * N_BRANCH, T), F32)],
        compiler_params=_cparams("arbitrary", "arbitrary"),
        name="nsa_project",
    )(x, wtok, wfeat, *tables)


def _compress_kernel(x_ref, pos_ref, w1_ref, w2_ref, o_ref):
    blocks = (x_ref[...].astype(F32) + pos_ref[...]).astype(BF16)
    pre = jnp.dot(blocks, w1_ref[...], preferred_element_type=F32)
    hid = jax.nn.gelu(pre, approximate=True)
    o_ref[...] = jnp.dot(hid.astype(BF16), w2_ref[...], preferred_element_type=F32).astype(BF16)


def nsa_compress(kv, pos_emb, w1, w2, tr=512):
    B, T, _ = kv.shape
    nch = T // CMP_STRIDE
    ch = kv.reshape(B, nch, CMP_STRIDE, N_KV_GROUPS, HEAD_DIM)
    blocks = jnp.concatenate([ch[:, :-1], ch[:, 1:]], axis=2)
    flat = jnp.moveaxis(blocks, 3, 2).reshape(B, nch - 1, N_KV_GROUPS, CMP_BLOCK * HEAD_DIM)
    flat = jnp.pad(flat, ((0, 0), (0, 1), (0, 0), (0, 0))).reshape(B * nch * N_KV_GROUPS, CMP_BLOCK * HEAD_DIM)
    rows = flat.shape[0]
    tr = min(tr, rows)
    const = lambda shape: pl.BlockSpec(shape, lambda i: (0,) * len(shape))
    out = pl.pallas_call(
        _compress_kernel,
        grid=(rows // tr,),
        in_specs=[pl.BlockSpec((tr, CMP_BLOCK * HEAD_DIM), lambda i: (i, 0)), const((1, CMP_BLOCK * HEAD_DIM)),
                  const((CMP_BLOCK * HEAD_DIM, CMP_HIDDEN)), const((CMP_HIDDEN, HEAD_DIM))],
        out_specs=pl.BlockSpec((tr, HEAD_DIM), lambda i: (i, 0)),
        out_shape=jax.ShapeDtypeStruct((rows, HEAD_DIM), BF16),
        compiler_params=_cparams("arbitrary"),
        name="nsa_compress",
    )(flat, pos_emb.reshape(1, -1).astype(F32), w1.astype(BF16), w2.astype(BF16))
    return out.reshape(B, nch, N_KV_GROUPS * HEAD_DIM)


def _softmax_keys(s, mask):
    s = jnp.where(mask, s, NEG_INF)
    m = jnp.max(s, axis=0, keepdims=True)
    m = jnp.where((m > NEG_INF) & (m < float("inf")), m, 0.0)
    e = jnp.exp2(s - m)
    d = jnp.sum(e, axis=0, keepdims=True)
    return e, 1.0 / jnp.where(d > 0, d, 1.0)


def _nsa_attn_kernel(q_ref, qr_ref, kc_ref, vct_ref, ks_ref, vst_ref, kw_ref, vwt_ref, g_ref, mt_ref, o_ref,
                     bias_ref, *, qb):
    t0 = pl.program_id(1) * qb
    hpg = HEADS_PER_GROUP
    t_row = t0 + lax.broadcasted_iota(I32, (1, qb), 1)
    t_all = jnp.concatenate([t_row] * hpg, axis=1)
    q = q_ref[...]
    qr_t = qr_ref[...].astype(F32).T
    gates = g_ref[...]
    n_cmp = kc_ref.shape[0]
    n_slc = mt_ref.shape[0]
    blocks_per_tile = SEL_KEYS // SLC_BLOCK
    o_cmp_all = []
    for g in range(N_KV_GROUPS):
        lanes = slice(g * HEAD_DIM, (g + 1) * HEAD_DIM)
        qg = jnp.concatenate(
            [q[:, (g * hpg + hp) * HEAD_DIM:(g * hpg + hp + 1) * HEAD_DIM] for hp in range(hpg)], axis=0)

        sc = lax.dot_general(kc_ref[:, lanes], qg, NT_DIMS, preferred_element_type=F32)
        cmp_end = lax.broadcasted_iota(I32, (n_cmp, 1), 0) * CMP_STRIDE + (CMP_BLOCK - 1)
        ec, inv_c = _softmax_keys(sc, cmp_end <= t_all)
        pc = ec * inv_c
        o_cmp_all.append(jnp.dot(vct_ref[lanes, :], pc.astype(BF16), preferred_element_type=F32))

        pcs = pc[:, :qb]
        for hp in range(1, hpg):
            pcs = pcs + pc[:, hp * qb:(hp + 1) * qb]
        hi = pcs.astype(BF16)
        r1 = pcs - hi.astype(F32)
        mid = r1.astype(BF16)
        low = (r1 - mid.astype(F32)).astype(BF16)
        mt = mt_ref[...]
        imp = (jnp.dot(mt, hi, preferred_element_type=F32) + jnp.dot(mt, mid, preferred_element_type=F32)
               + jnp.dot(mt, low, preferred_element_type=F32))
        j_io = lax.broadcasted_iota(I32, (n_slc, 1), 0)
        cur = t_row // SLC_BLOCK
        forced = (j_io == 0) | (j_io == cur) | (j_io == cur - 1)
        imp = jnp.where(forced, float("inf"), jnp.where(j_io * SLC_BLOCK <= t_row, imp, NEG_INF))
        sel = jnp.zeros((n_slc, qb), jnp.bool_)
        for _ in range(min(N_SELECT, n_slc)):
            m = jnp.max(imp, axis=0, keepdims=True)
            hit = j_io == jnp.min(jnp.where(imp == m, j_io, n_slc), axis=0, keepdims=True)
            sel = sel | hit
            imp = jnp.where(hit, NEG_INF, imp)
        bias_ref[g] = jnp.where(sel, 0.0, MASK_BIAS)

    q_ts = [jnp.concatenate([qr_t[(g * hpg + hp) * HEAD_DIM:(g * hpg + hp + 1) * HEAD_DIM, :]
                             for hp in range(hpg)], axis=1) for g in range(N_KV_GROUPS)]
    zero_rows = jnp.zeros((128 - HEAD_DIM - blocks_per_tile, hpg * qb), F32)

    def sel_tile(kt, carry, causal):
        k0 = pl.multiple_of(kt * SEL_KEYS, SEL_KEYS)
        b0 = pl.multiple_of(kt * blocks_per_tile, blocks_per_tile)
        out = []
        for g in range(N_KV_GROUPS):
            m_run, acc = carry[g]
            mask_rows = bias_ref[g, pl.ds(b0, blocks_per_tile), :]
            rhs = jnp.concatenate([q_ts[g], jnp.concatenate([mask_rows] * hpg, axis=1), zero_rows], axis=0)
            s = jnp.dot(ks_ref[pl.ds(k0, SEL_KEYS), g * 128:(g + 1) * 128], rhs.astype(BF16),
                        preferred_element_type=F32)
            if causal:
                kpos = k0 + lax.broadcasted_iota(I32, (SEL_KEYS, 1), 0)
                s = jnp.where(kpos <= t_all, s, MASK_BIAS)
            m_new = jnp.maximum(m_run, jnp.max(s, axis=0, keepdims=True))
            p = jnp.exp2(s - m_new)
            pv = jnp.dot(vst_ref[kt, g], p.astype(BF16), preferred_element_type=F32)
            out.append((m_new, jnp.exp2(m_run - m_new) * acc + pv))
        return tuple(out)

    init = tuple((jnp.full((1, hpg * qb), NEG_INF, F32), jnp.zeros((VS_ROWS, hpg * qb), F32))
                 for _ in range(N_KV_GROUPS))
    last = t0 // SEL_KEYS
    carry = lax.fori_loop(0, last, functools.partial(sel_tile, causal=False), init)
    sel_out = sel_tile(last, carry, True)

    heads = []
    for g in range(N_KV_GROUPS):
        lanes = slice(g * HEAD_DIM, (g + 1) * HEAD_DIM)
        q_t = q_ts[g]
        acc = sel_out[g][1]
        l_fin = acc[HEAD_DIM:HEAD_DIM + 1, :]
        o_sel = acc[:HEAD_DIM, :] * (1.0 / jnp.where(l_fin > 0, l_fin, 1.0))

        start = pl.multiple_of(jnp.maximum(t0 - WINDOW, 0), WIN_TILE)
        sw = jnp.dot(kw_ref[pl.ds(start, WIN_KEYS), lanes], q_t.astype(BF16), preferred_element_type=F32)
        dlt = t_all - (start + lax.broadcasted_iota(I32, (WIN_KEYS, 1), 0))
        ew, inv_w = _softmax_keys(sw, (dlt >= 0) & (dlt < WINDOW))
        w0 = start // WIN_TILE
        vwb = jnp.concatenate([vwt_ref[w0 + j, lanes, :] for j in range(WIN_KEYS // WIN_TILE)], axis=1)
        o_win = jnp.dot(vwb, ew.astype(BF16), preferred_element_type=F32) * inv_w

        o_cmp = o_cmp_all[g]
        for hp in range(hpg):
            r = (g * hpg + hp) * N_BRANCH
            cols = slice(hp * qb, (hp + 1) * qb)
            heads.append(gates[r:r + 1] * o_cmp[:, cols] + gates[r + 1:r + 2] * o_sel[:, cols]
                         + gates[r + 2:r + 3] * o_win[:, cols])
    o_ref[...] = jnp.concatenate(heads, axis=0).T.astype(BF16)


def nsa_attention(q, qr, kc, vct, ks, vst, kw, vwt, gates, qb=Q_TILE):
    B, T, nq = q.shape
    n_cmp = kc.shape[1]
    n_slc = T // SLC_BLOCK
    mt = np.zeros((n_slc, n_cmp), np.float32)
    per = SLC_BLOCK // CMP_STRIDE
    for j in range(n_slc):
        for k, wk in enumerate(SLC_OVERLAP_W):
            n = per * j + k - 1
            if 0 <= n < n_cmp - 1:
                mt[j, n] = wk
    per_b = lambda shape: pl.BlockSpec((None,) + shape, lambda b, i: (b,) + (0,) * len(shape))
    return pl.pallas_call(
        functools.partial(_nsa_attn_kernel, qb=qb),
        grid=(B, T // qb),
        in_specs=[pl.BlockSpec((None, qb, nq), lambda b, i: (b, i, 0)),
                  pl.BlockSpec((None, qb, nq), lambda b, i: (b, i, 0)),
                  per_b((n_cmp, KV_WIDTH)), per_b((KV_WIDTH, n_cmp)),
                  per_b((T, KS_WIDTH)), per_b((T // SEL_KEYS, N_KV_GROUPS, VS_ROWS, SEL_KEYS)),
                  per_b((T, KV_WIDTH)), per_b((T // WIN_TILE, KV_WIDTH, WIN_TILE)),
                  pl.BlockSpec((None, N_HEADS * N_BRANCH, qb), lambda b, i: (b, 0, i)),
                  pl.BlockSpec((n_slc, n_cmp), lambda b, i: (0, 0))],
        out_specs=pl.BlockSpec((None, qb, nq), lambda b, i: (b, i, 0)),
        out_shape=jax.ShapeDtypeStruct((B, T, nq), BF16),
        scratch_shapes=[pltpu.VMEM((N_KV_GROUPS, n_slc, qb), F32)],
        compiler_params=_cparams("arbitrary", "arbitrary"),
        name="nsa_attention",
    )(q, qr, kc, vct, ks, vst, kw, vwt, gates, jnp.asarray(mt, BF16))


def _proj_ln_kernel(a_ref, w_ref, x_ref, lng_ref, lnb_ref, o_ref):
    mix = jnp.dot(a_ref[...], w_ref[...], preferred_element_type=F32)
    o_ref[...] = _layer_norm(DEEPNORM_ALPHA * x_ref[...] + mix, lng_ref[...], lnb_ref[...])


def proj_ln(a, w, x, ln_g, ln_b, tq=512):
    N, D = x.shape
    K = a.shape[1]
    row = lambda v: v.reshape(1, -1).astype(F32)
    const = lambda shape: pl.BlockSpec(shape, lambda i: (0,) * len(shape))
    return pl.pallas_call(
        _proj_ln_kernel,
        grid=(N // tq,),
        in_specs=[pl.BlockSpec((tq, K), lambda i: (i, 0)), const((K, D)), pl.BlockSpec((tq, D), lambda i: (i, 0)),
                  const((1, D)), const((1, D))],
        out_specs=pl.BlockSpec((tq, D), lambda i: (i, 0)),
        out_shape=jax.ShapeDtypeStruct((N, D), F32),
        compiler_params=_cparams("arbitrary"),
        name="proj_ln",
    )(a, w.astype(BF16), x, row(ln_g), row(ln_b))


def nsa_layer(x, w_in, cmp_pos_k, cmp_w1_k, cmp_w2_k, cmp_pos_v, cmp_w1_v, cmp_w2_v, w_out, ln_g, ln_b):
    B, T, D = x.shape
    q, qr, kc, vc, ks, kw, vst, vwt, gates = nsa_project(x, w_in)
    kc_cmp = nsa_compress(kc, cmp_pos_k, cmp_w1_k, cmp_w2_k)
    vc_cmp = nsa_compress(vc, cmp_pos_v, cmp_w1_v, cmp_w2_v)
    o = nsa_attention(q, qr, kc_cmp, jnp.swapaxes(vc_cmp, 1, 2), ks, vst, kw, vwt, gates)
    return proj_ln(o.reshape(B * T, -1), w_out, x.reshape(B * T, D), ln_g, ln_b)


def kernel(x, ab_w_in, ab_pool_mix, ab_pool_scale, ab_conv_w, ab_conv_b, ab_conv_ln_g, ab_conv_ln_b, ab_w_out, nsa_w_in, nsa_cmp_pos_k, nsa_cmp_w1_k, nsa_cmp_w2_k, nsa_cmp_pos_v, nsa_cmp_w1_v, nsa_cmp_w2_v, nsa_w_out, ln_mix_g, ln_mix_b, ln_ffn_g, ln_ffn_b, moe_w_router, moe_b_router, moe_w_gate, moe_w_up, moe_w_down, moe_ws_gate, moe_ws_up, moe_ws_down):
    B, T, D = x.shape
    moe = lambda h, i: moe_layer(h, i, moe_w_router[i], moe_b_router[i], moe_w_gate, moe_w_up, moe_w_down,
                                 moe_ws_gate[i], moe_ws_up[i], moe_ws_down[i], ln_ffn_g[i], ln_ffn_b[i])
    h = ab_mixer(x, ab_w_in[0], ab_pool_mix[0], ab_pool_scale[0], ab_conv_w[0], ab_conv_b[0], ab_conv_ln_g[0],
                 ab_conv_ln_b[0], ab_w_out[0], ln_mix_g[0], ln_mix_b[0])
    h = moe(h.reshape(B * T, D), 0)
    h = nsa_layer(h.reshape(B, T, D), nsa_w_in[0], nsa_cmp_pos_k[0], nsa_cmp_w1_k[0], nsa_cmp_w2_k[0],
                  nsa_cmp_pos_v[0], nsa_cmp_w1_v[0], nsa_cmp_w2_v[0], nsa_w_out[0], ln_mix_g[1], ln_mix_b[1])
    h = moe(h, 1)
    return h.reshape(B, T, D)
```

```python
import functools

import jax
import jax.numpy as jnp
import numpy as np
from jax import lax
from jax.experimental import pallas as pl
from jax.experimental.pallas import tpu as pltpu

F32 = jnp.float32
BF16 = jnp.bfloat16
I32 = jnp.int32

D_MODEL = 1024
DEPTH = 2
DEEPNORM_ALPHA = (2.0 * DEPTH) ** 0.25
LN_EPS = 1e-5

POOL_WIDTH = 512
POOL_WINDOWS = (2, 4, 8, 16)
POOL_GROUP = 128
CONV_CH = 512
CONV_TAPS = 31
AB_IN = POOL_WIDTH + 2 * CONV_CH
HALO = 32
CONV_ROWS = 32

N_HEADS = 16
HEAD_DIM = 64
N_KV_GROUPS = 4
HEADS_PER_GROUP = 4
KV_WIDTH = 256
CMP_BLOCK = 32
CMP_STRIDE = 16
CMP_HIDDEN = 128
SLC_BLOCK = 64
N_SELECT = 16
WINDOW = 512
N_BRANCH = 3
ROPE_THETA = 500000.0
ROPE_DIM = 16
SLC_OVERLAP_W = (1.0, 2.0, 2.0, 2.0, 1.0)

N_EXPERTS = 256
TOP_K = 8
N_EXPERT_GROUPS = 8
TOPK_GROUPS = 4
EXPERT_FF = 256
SHARED_FF = 256
ROUTED_SCALE = 2.5
EXPERT_ROWS = 256
EXPERT_BUFFERS = 4
MOE_TQ = 256
SLOT_ROWS = 8
CHUNK_ROWS = 16
EXPERT_BLOCK = 32
TN_DIMS = (((0,), (0,)), ((), ()))

VMEM_LIMIT = 56 * 1024 * 1024
NEG_INF = float("-inf")


def _cparams(*sem):
    return pltpu.CompilerParams(dimension_semantics=sem, vmem_limit_bytes=VMEM_LIMIT)


def _layer_norm(y, g, b):
    mu = jnp.mean(y, axis=-1, keepdims=True)
    d = y - mu
    var = jnp.mean(d * d, axis=-1, keepdims=True)
    return d * lax.rsqrt(var + LN_EPS) * g + b


def _silu(v):
    return v * jax.nn.sigmoid(v)


def _ab_mixer_kernel(x_ref, win_ref, pmix_ref, pscale_ref, cw_ref, cb_ref, clg_ref, clb_ref,
                     wout_ref, lng_ref, lnb_ref, o_ref, eu_ref, ev_ref, cat_ref, *, tq):
    ti = pl.program_id(1)

    @pl.when(ti == 0)
    def _():
        eu_ref[0:HALO, :] = jnp.zeros((HALO, POOL_WIDTH), F32)
        ev_ref[0:HALO, :] = jnp.zeros((HALO, CONV_CH), F32)

    @pl.when(ti > 0)
    def _():
        eu_ref[0:HALO, :] = eu_ref[tq:tq + HALO, :]
        ev_ref[0:HALO, :] = ev_ref[tq:tq + HALO, :]

    x = x_ref[...]
    h = jnp.dot(x.astype(BF16), win_ref[...], preferred_element_type=F32)
    u = h[:, :POOL_WIDTH]
    eu_ref[HALO:HALO + tq, :] = u
    ev_ref[HALO:HALO + tq, :] = h[:, POOL_WIDTH:POOL_WIDTH + CONV_CH] * jax.nn.sigmoid(
        h[:, POOL_WIDTH + CONV_CH:])

    pos1 = ti * tq + lax.broadcasted_iota(I32, (tq, 1), 0) + 1
    for gi, w in enumerate(POOL_WINDOWS):
        lo, hi = gi * POOL_GROUP, (gi + 1) * POOL_GROUP
        ug = u[:, lo:hi]
        s = ug
        for j in range(1, w):
            s = s + eu_ref[HALO - j:HALO - j + tq, lo:hi]
        cnt = jnp.minimum(pos1, w).astype(F32)
        p = s / cnt - ug
        a = jnp.dot(p.astype(BF16), pmix_ref[gi], preferred_element_type=F32)
        cat_ref[:, lo:hi] = (a * pscale_ref[:, lo:hi]).astype(BF16)

    for r in range(tq // CONV_ROWS):
        base = r * CONV_ROWS + HALO - (CONV_TAPS - 1)
        acc = jnp.broadcast_to(cb_ref[...], (CONV_ROWS, CONV_CH))
        for k in range(CONV_TAPS):
            acc = acc + cw_ref[k:k + 1, :] * ev_ref[base + k:base + k + CONV_ROWS, :]
        c = _silu(_layer_norm(acc, clg_ref[...], clb_ref[...]))
        cat_ref[r * CONV_ROWS:(r + 1) * CONV_ROWS, POOL_WIDTH:] = c.astype(BF16)

    mix = jnp.dot(cat_ref[...], wout_ref[...], preferred_element_type=F32)
    o_ref[...] = _layer_norm(DEEPNORM_ALPHA * x + mix, lng_ref[...], lnb_ref[...])


def ab_mixer(x, w_in, pool_mix, pool_scale, conv_w, conv_b, conv_ln_g, conv_ln_b, w_out, ln_g, ln_b,
             tq=256):
    B, T, D = x.shape
    cw = jnp.pad(conv_w.reshape(CONV_TAPS, CONV_CH), ((0, 32 - CONV_TAPS), (0, 0)))
    row = lambda v: v.reshape(1, -1).astype(F32)
    const = lambda shape: pl.BlockSpec(shape, lambda b, t: (0,) * len(shape))
    return pl.pallas_call(
        functools.partial(_ab_mixer_kernel, tq=tq),
        grid=(B, T // tq),
        in_specs=[
            pl.BlockSpec((None, tq, D), lambda b, t: (b, t, 0)),
            const((D, AB_IN)), const((4, POOL_GROUP, POOL_GROUP)), const((1, POOL_WIDTH)),
            const((32, CONV_CH)), const((1, CONV_CH)), const((1, CONV_CH)), const((1, CONV_CH)),
            const((D, D)), const((1, D)), const((1, D)),
        ],
        out_specs=pl.BlockSpec((None, tq, D), lambda b, t: (b, t, 0)),
        out_shape=jax.ShapeDtypeStruct((B, T, D), F32),
        scratch_shapes=[pltpu.VMEM((HALO + tq, POOL_WIDTH), F32),
                        pltpu.VMEM((HALO + tq, CONV_CH), F32),
                        pltpu.VMEM((tq, D), BF16)],
        compiler_params=_cparams("arbitrary", "arbitrary"),
        name="ab_mixer",
    )(x, w_in.astype(BF16), pool_mix.astype(BF16), row(pool_scale), cw, row(conv_b),
      row(conv_ln_g), row(conv_ln_b), w_out.astype(BF16), row(ln_g), row(ln_b))


def _first_index(hit_src, m, iota, size):
    return jnp.min(jnp.where(hit_src == m, iota, size), axis=0, keepdims=True)


def _router_kernel(x_ref, wh_ref, wl_ref, b_ref, tri_ref, idx_ref, wt_ref, lrank_ref, cnt_ref, *, tq):
    x = x_ref[...]
    xh = x.astype(BF16)
    xl = (x - xh.astype(F32)).astype(BF16)
    nt = (((1,), (1,)), ((), ()))
    wh, wl = wh_ref[...], wl_ref[...]
    logits = (lax.dot_general(wh, xh, nt, preferred_element_type=F32)
              + lax.dot_general(wh, xl, nt, preferred_element_type=F32)
              + lax.dot_general(wl, xh, nt, preferred_element_type=F32))
    scores = jax.nn.sigmoid(logits)
    biased = scores + b_ref[...]

    per_group = N_EXPERTS // N_EXPERT_GROUPS
    io_g = lax.broadcasted_iota(I32, (per_group, tq), 0)
    gs_rows = []
    for g in range(N_EXPERT_GROUPS):
        sg = biased[g * per_group:(g + 1) * per_group, :]
        m1 = jnp.max(sg, axis=0, keepdims=True)
        i1 = _first_index(sg, m1, io_g, per_group)
        m2 = jnp.max(jnp.where(io_g == i1, NEG_INF, sg), axis=0, keepdims=True)
        gs_rows.append(m1 + m2)
    gs = jnp.concatenate(gs_rows, axis=0)

    io_ng = lax.broadcasted_iota(I32, (N_EXPERT_GROUPS, tq), 0)
    grp_sel = jnp.zeros((N_EXPERT_GROUPS, tq), jnp.bool_)
    for _ in range(TOPK_GROUPS):
        m = jnp.max(gs, axis=0, keepdims=True)
        hit = io_ng == _first_index(gs, m, io_ng, N_EXPERT_GROUPS)
        grp_sel = grp_sel | hit
        gs = jnp.where(hit, NEG_INF, gs)
    exp_mask = jnp.concatenate(
        [jnp.broadcast_to(grp_sel[g:g + 1, :], (per_group, tq)) for g in range(N_EXPERT_GROUPS)], axis=0)

    masked = jnp.where(exp_mask, biased, NEG_INF)
    io_e = lax.broadcasted_iota(I32, (N_EXPERTS, tq), 0)
    idx_rows, w_rows = [], []
    onehot = jnp.zeros((N_EXPERTS, tq), F32)
    for _ in range(TOP_K):
        m = jnp.max(masked, axis=0, keepdims=True)
        first = _first_index(masked, m, io_e, N_EXPERTS)
        hit = io_e == first
        idx_rows.append(first)
        w_rows.append(jnp.sum(jnp.where(hit, scores, 0.0), axis=0, keepdims=True))
        onehot = onehot + hit.astype(F32)
        masked = jnp.where(hit, NEG_INF, masked)
    w = jnp.concatenate(w_rows, axis=0)
    wt_ref[...] = w / jnp.sum(w, axis=0, keepdims=True) * ROUTED_SCALE
    idx_ref[...] = jnp.concatenate(idx_rows, axis=0)

    oh = onehot.astype(BF16)
    before = jnp.dot(oh, tri_ref[...], preferred_element_type=F32)
    lrank_ref[...] = jnp.concatenate(
        [jnp.sum(jnp.where(io_e == r, before, 0.0), axis=0, keepdims=True) for r in idx_rows],
        axis=0).astype(I32)
    cnt_ref[...] = lax.dot_general(jnp.ones((8, tq), BF16), oh, nt, preferred_element_type=F32).astype(I32)


def moe_router(x, w_router, b_router, tq):
    N, D = x.shape
    wt = w_router.T
    wh = wt.astype(BF16)
    wl = (wt - wh.astype(F32)).astype(BF16)
    tri = (np.arange(tq)[:, None] < np.arange(tq)[None, :]).astype(np.float32)
    const = lambda shape: pl.BlockSpec(shape, lambda i: (0,) * len(shape))
    tok = lambda rows: pl.BlockSpec((rows, tq), lambda i: (0, i))
    return pl.pallas_call(
        functools.partial(_router_kernel, tq=tq),
        grid=(N // tq,),
        in_specs=[pl.BlockSpec((tq, D), lambda i: (i, 0)), const((N_EXPERTS, D)), const((N_EXPERTS, D)),
                  const((N_EXPERTS, 1)), const((tq, tq))],
        out_specs=[tok(TOP_K), tok(TOP_K), tok(TOP_K), pl.BlockSpec((None, 8, N_EXPERTS), lambda i: (i, 0, 0))],
        out_shape=[jax.ShapeDtypeStruct((TOP_K, N), I32), jax.ShapeDtypeStruct((TOP_K, N), F32),
                   jax.ShapeDtypeStruct((TOP_K, N), I32), jax.ShapeDtypeStruct((N // tq, 8, N_EXPERTS), I32)],
        compiler_params=_cparams("arbitrary"),
        name="moe_router",
    )(x, wh, wl, b_router.reshape(N_EXPERTS, 1).astype(F32), jnp.asarray(tri, BF16))


def _pack_pairs(v):
    half = v.shape[1] // 2
    hi = lax.bitcast_convert_type(v[:, :half], I32)
    lo = lax.bitcast_convert_type(v[:, half:], I32)
    return (hi & jnp.int32(-65536)) | lax.shift_right_logical(lo, jnp.int32(16))


def _unpack_pairs(p):
    hi = lax.bitcast_convert_type(p & jnp.int32(-65536), F32)
    lo = lax.bitcast_convert_type(lax.shift_left(p, jnp.int32(16)), F32)
    return jnp.concatenate([hi, lo], axis=1).astype(BF16)


def _round_bf16(v):
    return v.astype(BF16).astype(F32)


def _slot_copies(cnt_ref, off_ref, hbm, buf, sem, ovf_ref, to_hbm):
    def issue(e, carry):
        n_small, n_big, n_ovf = carry
        c = cnt_ref[0, 0, e]
        off = pl.multiple_of(off_ref[0, 0, e], SLOT_ROWS)
        small = (c > 0) & (c <= SLOT_ROWS)
        big = c > SLOT_ROWS
        over = c > CHUNK_ROWS
        base = pl.multiple_of(e * CHUNK_ROWS, CHUNK_ROWS)

        @pl.when(over)
        def _():
            ovf_ref[n_ovf] = e

        @pl.when(small)
        def _():
            pair = (buf.at[pl.ds(base, SLOT_ROWS), :], hbm.at[pl.ds(off, SLOT_ROWS), :])
            pltpu.make_async_copy(*(pair if to_hbm else pair[::-1]), sem).start()

        @pl.when(big)
        def _():
            pair = (buf.at[pl.ds(base, CHUNK_ROWS), :], hbm.at[pl.ds(off, CHUNK_ROWS), :])
            pltpu.make_async_copy(*(pair if to_hbm else pair[::-1]), sem).start()
        return n_small + small.astype(I32), n_big + big.astype(I32), n_ovf + over.astype(I32)
    return lax.fori_loop(0, N_EXPERTS, issue, (0, 0, 0))


def _slot_waits(n_small, n_big, hbm, buf, sem, to_hbm):
    def drain(rows):
        def body(_, c):
            pair = (buf.at[pl.ds(0, rows), :], hbm.at[pl.ds(0, rows), :])
            pltpu.make_async_copy(*(pair if to_hbm else pair[::-1]), sem).wait()
            return c
        return body
    lax.fori_loop(0, n_small, drain(SLOT_ROWS), 0)
    lax.fori_loop(0, n_big, drain(CHUNK_ROWS), 0)


def _slot_rows(lrk_at, fill, tq):
    io_r = lax.broadcasted_iota(I32, (CHUNK_ROWS, tq), 0)
    blocks = []
    for eb in range(N_EXPERTS // EXPERT_BLOCK):
        rows = []
        for e in range(eb * EXPERT_BLOCK, (eb + 1) * EXPERT_BLOCK):
            hit = jnp.broadcast_to(lrk_at[e:e + 1, :], (CHUNK_ROWS, tq)) == io_r
            rows.append(fill(e, hit))
        blocks.append(jnp.concatenate(rows, axis=0))
    return blocks


def _dispatch_kernel(cnt_ref, off_ref, x_ref, idx_ref, lrk_ref, xs_hbm, xbuf, obuf, ovf_ref, sem, *, tq):
    xb = x_ref[...].astype(BF16)
    io_e = lax.broadcasted_iota(I32, (N_EXPERTS, tq), 0)
    lrk_at = jnp.full((N_EXPERTS, tq), -1, I32)
    for k in range(TOP_K):
        lrk_at = jnp.where(io_e == idx_ref[k:k + 1, :], lrk_ref[k:k + 1, :], lrk_at)
    rows_per_block = EXPERT_BLOCK * CHUNK_ROWS
    for eb, pick in enumerate(_slot_rows(lrk_at, lambda e, hit: hit.astype(BF16), tq)):
        xbuf[eb * rows_per_block:(eb + 1) * rows_per_block, :] = _pack_pairs(
            jnp.dot(pick, xb, preferred_element_type=F32))
    n_small, n_big, n_ovf = _slot_copies(cnt_ref, off_ref, xs_hbm, xbuf, sem, ovf_ref, True)
    _slot_waits(n_small, n_big, xs_hbm, xbuf, sem, True)

    io_r = lax.broadcasted_iota(I32, (SLOT_ROWS, tq), 0)

    def overflow(i, c):
        e = ovf_ref[i]
        n_units = (cnt_ref[0, 0, e] + SLOT_ROWS - 1) // SLOT_ROWS
        off = pl.multiple_of(off_ref[0, 0, e], SLOT_ROWS)

        def extra(j, c2):
            rank_e = jnp.full((1, tq), -1, I32)
            for k in range(TOP_K):
                rank_e = jnp.where(idx_ref[k:k + 1, :] == e, lrk_ref[k:k + 1, :], rank_e)
            pick = (io_r + j * SLOT_ROWS == rank_e).astype(BF16)
            obuf[...] = _pack_pairs(jnp.dot(pick, xb, preferred_element_type=F32))
            cp = pltpu.make_async_copy(obuf, xs_hbm.at[pl.ds(off + j * SLOT_ROWS, SLOT_ROWS), :], sem)
            cp.start()
            cp.wait()
            return c2
        return lax.fori_loop(CHUNK_ROWS // SLOT_ROWS, n_units, extra, c)
    lax.fori_loop(0, n_ovf, overflow, 0)


def moe_dispatch(x, idx, lrank, tile_cnt, tile_off, n_rows, tq):
    N, D = x.shape
    n_tiles = N // tq
    smem = lambda: pl.BlockSpec((1, 1, N_EXPERTS), lambda i: (i, 0, 0), memory_space=pltpu.SMEM)
    return pl.pallas_call(
        functools.partial(_dispatch_kernel, tq=tq),
        grid=(n_tiles,),
        in_specs=[smem(), smem(), pl.BlockSpec((tq, D), lambda i: (i, 0)),
                  pl.BlockSpec((TOP_K, tq), lambda i: (0, i)), pl.BlockSpec((TOP_K, tq), lambda i: (0, i))],
        out_specs=pl.BlockSpec(memory_space=pl.ANY),
        out_shape=jax.ShapeDtypeStruct((n_rows, D // 2), I32),
        scratch_shapes=[pltpu.VMEM((N_EXPERTS * CHUNK_ROWS, D // 2), I32), pltpu.VMEM((SLOT_ROWS, D // 2), I32),
                        pltpu.SMEM((N_EXPERTS,), I32), pltpu.SemaphoreType.DMA(())],
        compiler_params=_cparams("arbitrary"),
        name="moe_dispatch",
    )(tile_cnt.reshape(n_tiles, 1, N_EXPERTS), tile_off.reshape(n_tiles, 1, N_EXPERTS), x, idx, lrank)


def _expert_kernel(row0_ref, nblk_ref, used_ref, x_hbm, wg_ref, wu_ref, wd_ref, y_hbm, wgb, wub, wdb, xin, yout,
                   in_sem, out_sem):
    e = pl.program_id(0)
    m = EXPERT_ROWS
    nb = nblk_ref[e]
    row0 = row0_ref[e]
    rows_of = lambda j: pl.ds(pl.multiple_of(row0 + j * m, m), m)
    in_copy = lambda j, slot: pltpu.make_async_copy(x_hbm.at[rows_of(j), :], xin.at[slot], in_sem.at[slot])
    out_copy = lambda j, slot: pltpu.make_async_copy(yout.at[slot], y_hbm.at[rows_of(j), :], out_sem.at[slot])

    @pl.when(nb > 0)
    def _():
        for j0 in range(EXPERT_BUFFERS - 1):
            @pl.when(j0 < nb)
            def _(j0=j0):
                in_copy(j0, j0).start()
        wgb[...] = wg_ref[...].astype(BF16)
        wub[...] = wu_ref[...].astype(BF16)
        wdb[...] = wd_ref[...].astype(BF16)

        def block(j, c):
            slot = j % EXPERT_BUFFERS
            in_copy(j, slot).wait()
            ahead = j + EXPERT_BUFFERS - 1

            @pl.when(ahead < nb)
            def _():
                in_copy(ahead, ahead % EXPERT_BUFFERS).start()

            @pl.when(j >= EXPERT_BUFFERS)
            def _():
                out_copy(j - EXPERT_BUFFERS, slot).wait()

            rows = lax.broadcasted_iota(I32, (m, xin.shape[2]), 0)
            xb = _unpack_pairs(jnp.where(rows < used_ref[e] - j * m, xin[slot], 0))
            g = jnp.dot(xb, wgb[...], preferred_element_type=F32)
            u = jnp.dot(xb, wub[...], preferred_element_type=F32)
            hid = (_silu(g) * u).astype(BF16)
            yout[slot] = _pack_pairs(_round_bf16(jnp.dot(hid, wdb[...], preferred_element_type=F32)))
            out_copy(j, slot).start()
            return c
        lax.fori_loop(0, nb, block, 0)

        for back in range(1, EXPERT_BUFFERS + 1):
            @pl.when(nb >= back)
            def _(back=back):
                out_copy(nb - back, (nb - back) % EXPERT_BUFFERS).wait()


def moe_experts(xs, row0, nblk, used, w_gate, w_up, w_down, layer):
    m = EXPERT_ROWS
    D = xs.shape[1] * 2
    wspec = lambda r, c: pl.BlockSpec((None, None, r, c), lambda e, *_: (layer, e, 0, 0))
    grid_spec = pltpu.PrefetchScalarGridSpec(
        num_scalar_prefetch=3,
        grid=(N_EXPERTS,),
        in_specs=[pl.BlockSpec(memory_space=pl.ANY), wspec(D, EXPERT_FF), wspec(D, EXPERT_FF), wspec(EXPERT_FF, D)],
        out_specs=pl.BlockSpec(memory_space=pl.ANY),
        scratch_shapes=[pltpu.VMEM((D, EXPERT_FF), BF16), pltpu.VMEM((D, EXPERT_FF), BF16),
                        pltpu.VMEM((EXPERT_FF, D), BF16), pltpu.VMEM((EXPERT_BUFFERS, m, D // 2), I32),
                        pltpu.VMEM((EXPERT_BUFFERS, m, D // 2), I32), pltpu.SemaphoreType.DMA((EXPERT_BUFFERS,)),
                        pltpu.SemaphoreType.DMA((EXPERT_BUFFERS,))],
    )
    return pl.pallas_call(
        _expert_kernel,
        grid_spec=grid_spec,
        out_shape=jax.ShapeDtypeStruct(xs.shape, I32),
        compiler_params=_cparams("arbitrary"),
        name="moe_experts",
    )(row0, nblk, used, xs, w_gate, w_up, w_down)


def _combine_kernel(cnt_ref, off_ref, y_hbm, idx_ref, lrk_ref, w_ref, x_ref, wsg_ref, wsu_ref, wsd_ref,
                    lng_ref, lnb_ref, o_ref, ybuf, obuf, acc_ref, ovf_ref, sem, *, tq):
    @pl.when(pl.program_id(0) == 0)
    def _():
        ybuf[...] = jnp.zeros_like(ybuf)

    n_small, n_big, n_ovf = _slot_copies(cnt_ref, off_ref, y_hbm, ybuf, sem, ovf_ref, False)

    x = x_ref[...]
    xb = x.astype(BF16)
    g = jnp.dot(xb, wsg_ref[...], preferred_element_type=F32)
    u = jnp.dot(xb, wsu_ref[...], preferred_element_type=F32)
    acc_ref[...] = jnp.dot((_silu(g) * u).astype(BF16), wsd_ref[...], preferred_element_type=F32)

    io_e = lax.broadcasted_iota(I32, (N_EXPERTS, tq), 0)
    w_at = jnp.zeros((N_EXPERTS, tq), F32)
    lrk_at = jnp.full((N_EXPERTS, tq), -1, I32)
    for k in range(TOP_K):
        hit = io_e == idx_ref[k:k + 1, :]
        w_at = jnp.where(hit, w_ref[k:k + 1, :], w_at)
        lrk_at = jnp.where(hit, lrk_ref[k:k + 1, :], lrk_at)
    fill = lambda e, hit: jnp.where(hit, jnp.broadcast_to(w_at[e:e + 1, :], hit.shape), 0.0).astype(BF16)
    weights = _slot_rows(lrk_at, fill, tq)

    _slot_waits(n_small, n_big, y_hbm, ybuf, sem, False)

    rows_per_block = EXPERT_BLOCK * CHUNK_ROWS
    for eb, wct in enumerate(weights):
        yb = _unpack_pairs(ybuf[eb * rows_per_block:(eb + 1) * rows_per_block, :])
        acc_ref[...] += lax.dot_general(wct, yb, TN_DIMS, preferred_element_type=F32)

    io_r = lax.broadcasted_iota(I32, (SLOT_ROWS, tq), 0)

    def overflow(i, c):
        e = ovf_ref[i]
        n_units = (cnt_ref[0, 0, e] + SLOT_ROWS - 1) // SLOT_ROWS
        off = pl.multiple_of(off_ref[0, 0, e], SLOT_ROWS)

        def extra(j, c2):
            cp = pltpu.make_async_copy(y_hbm.at[pl.ds(off + j * SLOT_ROWS, SLOT_ROWS), :], obuf, sem)
            cp.start()
            w_e = jnp.zeros((1, tq), F32)
            rank_e = jnp.full((1, tq), -1, I32)
            for k in range(TOP_K):
                hit = idx_ref[k:k + 1, :] == e
                w_e = jnp.where(hit, w_ref[k:k + 1, :], w_e)
                rank_e = jnp.where(hit, lrk_ref[k:k + 1, :], rank_e)
            wct = jnp.where(io_r + j * SLOT_ROWS == rank_e, w_e, 0.0).astype(BF16)
            cp.wait()
            acc_ref[...] += lax.dot_general(wct, _unpack_pairs(obuf[...]), TN_DIMS, preferred_element_type=F32)
            return c2
        return lax.fori_loop(CHUNK_ROWS // SLOT_ROWS, n_units, extra, c)
    lax.fori_loop(0, n_ovf, overflow, 0)

    o_ref[...] = _layer_norm(DEEPNORM_ALPHA * x + acc_ref[...], lng_ref[...], lnb_ref[...])


def moe_combine(x, ys, idx, lrank, wts, tile_cnt, tile_off, ws_gate, ws_up, ws_down, ln_g, ln_b, tq):
    N, D = x.shape
    n_tiles = N // tq
    row = lambda v: v.reshape(1, -1).astype(F32)
    const = lambda shape: pl.BlockSpec(shape, lambda i: (0,) * len(shape))
    smem = lambda: pl.BlockSpec((1, 1, N_EXPERTS), lambda i: (i, 0, 0), memory_space=pltpu.SMEM)
    tokl = lambda: pl.BlockSpec((TOP_K, tq), lambda i: (0, i))
    return pl.pallas_call(
        functools.partial(_combine_kernel, tq=tq),
        grid=(n_tiles,),
        in_specs=[smem(), smem(), pl.BlockSpec(memory_space=pl.ANY), tokl(), tokl(), tokl(),
                  pl.BlockSpec((tq, D), lambda i: (i, 0)),
                  const((D, SHARED_FF)), const((D, SHARED_FF)), const((SHARED_FF, D)), const((1, D)), const((1, D))],
        out_specs=pl.BlockSpec((tq, D), lambda i: (i, 0)),
        out_shape=jax.ShapeDtypeStruct((N, D), F32),
        scratch_shapes=[pltpu.VMEM((N_EXPERTS * CHUNK_ROWS, D // 2), I32), pltpu.VMEM((SLOT_ROWS, D // 2), I32),
                        pltpu.VMEM((tq, D), F32), pltpu.SMEM((N_EXPERTS,), I32), pltpu.SemaphoreType.DMA(())],
        compiler_params=_cparams("arbitrary"),
        name="moe_combine",
    )(tile_cnt.reshape(n_tiles, 1, N_EXPERTS), tile_off.reshape(n_tiles, 1, N_EXPERTS), ys, idx, lrank,
      wts, x, ws_gate.astype(BF16), ws_up.astype(BF16), ws_down.astype(BF16), row(ln_g), row(ln_b))


def moe_layer(x, layer, w_router, b_router, w_gate, w_up, w_down, ws_gate, ws_up, ws_down, ln_g, ln_b):
    N, D = x.shape
    m = EXPERT_ROWS
    tq = MOE_TQ
    idx, wts, lrank, tile_cnt = moe_router(x, w_router, b_router, tq)
    tile_cnt = tile_cnt[:, 0, :]
    slots = (tile_cnt + SLOT_ROWS - 1) // SLOT_ROWS * SLOT_ROWS
    used = jnp.sum(slots, axis=0)
    padded = (used + m - 1) // m * m
    pend = jnp.cumsum(padded)
    pstart = pend - padded
    tile_off = pstart[None, :] + jnp.cumsum(slots, axis=0) - slots
    n_assign = N * TOP_K
    max_rows = n_assign + (SLOT_ROWS - 1) * min(n_assign, (N // tq) * N_EXPERTS) + N_EXPERTS * (m - 1)
    xs = moe_dispatch(x, idx, lrank, tile_cnt, tile_off, -(-max_rows // m) * m, tq)
    ys = moe_experts(xs, pstart.astype(I32), (padded // m).astype(I32), used.astype(I32), w_gate, w_up, w_down, layer)
    return moe_combine(x, ys, idx, lrank, wts, tile_cnt, tile_off, ws_gate, ws_up, ws_down, ln_g, ln_b, tq)


Q_TILE = 128
SEL_KEYS = 512
WIN_TILE = 128
WIN_KEYS = WINDOW + Q_TILE
PROJ_TQ = SEL_KEYS
KS_WIDTH = N_KV_GROUPS * 128
VS_ROWS = HEAD_DIM + 16
NSA_TOK_COLS = N_HEADS * HEAD_DIM + 3 * KV_WIDTH + KS_WIDTH
NSA_FEAT_ROWS = 2 * KV_WIDTH + N_HEADS * N_BRANCH
NT_DIMS = (((1,), (1,)), ((), ()))
LOG2E = 1.4426950408889634
MASK_BIAS = -1e30


def _rotary(v, cos, sin, lo_mask):
    parts = []
    for c in range(v.shape[1] // 128):
        blk = v[:, c * 128:(c + 1) * 128]
        nxt = pltpu.roll(blk, 128 - ROPE_DIM // 2, axis=1)
        prv = pltpu.roll(blk, ROPE_DIM // 2, axis=1)
        parts.append(blk * cos + jnp.where(lo_mask, nxt, prv) * sin)
    return jnp.concatenate(parts, axis=1)


def _nsa_proj_kernel(x_ref, wtok_ref, wfeat_ref, cos_ref, sin_ref, cosk_ref, sink_ref, q_ref, qr_ref, kc_ref,
                     vc_ref, ks_ref, kw_ref, vs_ref, vw_ref, g_ref, *, tq):
    xb = x_ref[...].astype(BF16)
    h = jnp.dot(xb, wtok_ref[...], preferred_element_type=F32)
    cos, sin = cos_ref[...], sin_ref[...]
    lane = lax.broadcasted_iota(I32, (tq, 128), 1)
    lo_mask = (lane % HEAD_DIM) < ROPE_DIM // 2
    nq = N_HEADS * HEAD_DIM
    q = h[:, :nq] * (HEAD_DIM ** -0.5 * LOG2E)
    q_ref[...] = q.astype(BF16)
    qr_ref[...] = _rotary(q, cos, sin, lo_mask).astype(BF16)
    c0 = nq
    kc_ref[...] = h[:, c0:c0 + KV_WIDTH].astype(BF16)
    vc_ref[...] = h[:, c0 + KV_WIDTH:c0 + 2 * KV_WIDTH].astype(BF16)
    c0 += 2 * KV_WIDTH
    ks = _rotary(h[:, c0:c0 + N_KV_GROUPS * 128], cosk_ref[...], sink_ref[...], lo_mask)
    blk = lax.broadcasted_iota(I32, (tq, 128), 0) % SEL_KEYS // SLC_BLOCK
    onehot = (lane - HEAD_DIM == blk).astype(F32)
    ks_ref[...] = (ks + jnp.concatenate([onehot] * N_KV_GROUPS, axis=1)).astype(BF16)
    c0 += N_KV_GROUPS * 128
    kw_ref[...] = _rotary(h[:, c0:c0 + KV_WIDTH], cos, sin, lo_mask).astype(BF16)
    hf = lax.dot_general(wfeat_ref[...], xb, NT_DIMS, preferred_element_type=F32)
    ones_row = (lax.broadcasted_iota(I32, (VS_ROWS - HEAD_DIM, tq), 0) == 0).astype(BF16)
    for g in range(N_KV_GROUPS):
        vs_ref[g, 0:HEAD_DIM, :] = hf[g * HEAD_DIM:(g + 1) * HEAD_DIM].astype(BF16)
        vs_ref[g, HEAD_DIM:VS_ROWS, :] = ones_row
    for j in range(tq // WIN_TILE):
        vw_ref[j] = hf[KV_WIDTH:2 * KV_WIDTH, j * WIN_TILE:(j + 1) * WIN_TILE].astype(BF16)
    g_ref[...] = jax.nn.sigmoid(hf[2 * KV_WIDTH:])


def nsa_project(x, w_in, tq=PROJ_TQ):
    B, T, D = x.shape
    nq = N_HEADS * HEAD_DIM
    q, kc, vc, ks, vs, kw, vw, g = jnp.split(w_in, [nq + i * KV_WIDTH for i in range(7)], axis=1)
    ks_wide = jnp.pad(ks.reshape(D, N_KV_GROUPS, HEAD_DIM), ((0, 0), (0, 0), (0, 128 - HEAD_DIM)))
    wtok = jnp.concatenate([q, kc, vc, ks_wide.reshape(D, KS_WIDTH), kw], axis=1).astype(BF16)
    wfeat = jnp.concatenate([vs, vw, g], axis=1).T.astype(BF16)
    inv_freq = jnp.power(jnp.float32(ROPE_THETA), -jnp.arange(0, ROPE_DIM, 2, dtype=F32) / ROPE_DIM)
    ang = jnp.arange(T).astype(F32)[:, None] * inv_freq[None, :]
    ones = jnp.ones((T, HEAD_DIM - ROPE_DIM), F32)
    cos_h = jnp.concatenate([jnp.cos(ang), jnp.cos(ang), ones], axis=1)
    sin_h = jnp.concatenate([-jnp.sin(ang), jnp.sin(ang), 0.0 * ones], axis=1)
    ident = jnp.ones((T, 128 - HEAD_DIM), F32)
    tables = [jnp.concatenate([cos_h, cos_h], axis=1), jnp.concatenate([sin_h, sin_h], axis=1),
              jnp.concatenate([cos_h, ident], axis=1), jnp.concatenate([sin_h, 0.0 * ident], axis=1)]
    nt = T // tq
    tokm = lambda w: pl.BlockSpec((None, tq, w), lambda b, t: (b, t, 0))
    const = lambda shape: pl.BlockSpec(shape, lambda b, t: (0,) * len(shape))
    table = lambda: pl.BlockSpec((tq, 128), lambda b, t: (t, 0))
    bf = lambda *shape: jax.ShapeDtypeStruct(shape, BF16)
    return pl.pallas_call(
        functools.partial(_nsa_proj_kernel, tq=tq),
        grid=(B, nt),
        in_specs=[tokm(D), const((D, NSA_TOK_COLS)), const((NSA_FEAT_ROWS, D)), table(), table(), table(), table()],
        out_specs=[tokm(nq), tokm(nq), tokm(KV_WIDTH), tokm(KV_WIDTH), tokm(KS_WIDTH), tokm(KV_WIDTH),
                   pl.BlockSpec((None, None, N_KV_GROUPS, VS_ROWS, tq), lambda b, t: (b, t, 0, 0, 0)),
                   pl.BlockSpec((None, tq // WIN_TILE, KV_WIDTH, WIN_TILE), lambda b, t: (b, t, 0, 0)),
                   pl.BlockSpec((None, N_HEADS * N_BRANCH, tq), lambda b, t: (b, 0, t))],
        out_shape=[bf(B, T, nq), bf(B, T, nq), bf(B, T, KV_WIDTH), bf(B, T, KV_WIDTH), bf(B, T, KS_WIDTH),
                   bf(B, T, KV_WIDTH), bf(B, T // tq, N_KV_GROUPS, VS_ROWS, tq),
                   bf(B, T // WIN_TILE, KV_WIDTH, WIN_TILE),
                   jax.ShapeDtypeStruct((B, N_HEADS * N_BRANCH, T), F32)],
        compiler_params=_cparams("arbitrary", "arbitrary"),
        name="nsa_project",
    )(x, wtok, wfeat, *tables)


def _compress_kernel(x_ref, pos_ref, w1_ref, w2_ref, o_ref):
    blocks = (x_ref[...].astype(F32) + pos_ref[...]).astype(BF16)
    pre = jnp.dot(blocks, w1_ref[...], preferred_element_type=F32)
    hid = jax.nn.gelu(pre, approximate=True)
    o_ref[...] = jnp.dot(hid.astype(BF16), w2_ref[...], preferred_element_type=F32).astype(BF16)


def nsa_compress(kv, pos_emb, w1, w2, tr=512):
    B, T, _ = kv.shape
    nch = T // CMP_STRIDE
    ch = kv.reshape(B, nch, CMP_STRIDE, N_KV_GROUPS, HEAD_DIM)
    blocks = jnp.concatenate([ch[:, :-1], ch[:, 1:]], axis=2)
    flat = jnp.moveaxis(blocks, 3, 2).reshape(B, nch - 1, N_KV_GROUPS, CMP_BLOCK * HEAD_DIM)
    flat = jnp.pad(flat, ((0, 0), (0, 1), (0, 0), (0, 0))).reshape(B * nch * N_KV_GROUPS, CMP_BLOCK * HEAD_DIM)
    rows = flat.shape[0]
    tr = min(tr, rows)
    const = lambda shape: pl.BlockSpec(shape, lambda i: (0,) * len(shape))
    out = pl.pallas_call(
        _compress_kernel,
        grid=(rows // tr,),
        in_specs=[pl.BlockSpec((tr, CMP_BLOCK * HEAD_DIM), lambda i: (i, 0)), const((1, CMP_BLOCK * HEAD_DIM)),
                  const((CMP_BLOCK * HEAD_DIM, CMP_HIDDEN)), const((CMP_HIDDEN, HEAD_DIM))],
        out_specs=pl.BlockSpec((tr, HEAD_DIM), lambda i: (i, 0)),
        out_shape=jax.ShapeDtypeStruct((rows, HEAD_DIM), BF16),
        compiler_params=_cparams("arbitrary"),
        name="nsa_compress",
    )(flat, pos_emb.reshape(1, -1).astype(F32), w1.astype(BF16), w2.astype(BF16))
    return out.reshape(B, nch, N_KV_GROUPS * HEAD_DIM)


def _softmax_keys(s, mask):
    s = jnp.where(mask, s, NEG_INF)
    m = jnp.max(s, axis=0, keepdims=True)
    m = jnp.where((m > NEG_INF) & (m < float("inf")), m, 0.0)
    e = jnp.exp2(s - m)
    d = jnp.sum(e, axis=0, keepdims=True)
    return e, 1.0 / jnp.where(d > 0, d, 1.0)


def _nsa_attn_kernel(q_ref, qr_ref, kc_ref, vct_ref, ks_ref, vst_ref, kw_ref, vwt_ref, g_ref, mt_ref, o_ref,
                     bias_ref, *, qb):
    t0 = pl.program_id(1) * qb
    hpg = HEADS_PER_GROUP
    t_row = t0 + lax.broadcasted_iota(I32, (1, qb), 1)
    t_all = jnp.concatenate([t_row] * hpg, axis=1)
    q = q_ref[...]
    qr_t = qr_ref[...].astype(F32).T
    gates = g_ref[...]
    n_cmp = kc_ref.shape[0]
    n_slc = mt_ref.shape[0]
    blocks_per_tile = SEL_KEYS // SLC_BLOCK
    o_cmp_all = []
    for g in range(N_KV_GROUPS):
        lanes = slice(g * HEAD_DIM, (g + 1) * HEAD_DIM)
        qg = jnp.concatenate(
            [q[:, (g * hpg + hp) * HEAD_DIM:(g * hpg + hp + 1) * HEAD_DIM] for hp in range(hpg)], axis=0)

        sc = lax.dot_general(kc_ref[:, lanes], qg, NT_DIMS, preferred_element_type=F32)
        cmp_end = lax.broadcasted_iota(I32, (n_cmp, 1), 0) * CMP_STRIDE + (CMP_BLOCK - 1)
        ec, inv_c = _softmax_keys(sc, cmp_end <= t_all)
        pc = ec * inv_c
        o_cmp_all.append(jnp.dot(vct_ref[lanes, :], pc.astype(BF16), preferred_element_type=F32))

        pcs = pc[:, :qb]
        for hp in range(1, hpg):
            pcs = pcs + pc[:, hp * qb:(hp + 1) * qb]
        hi = pcs.astype(BF16)
        r1 = pcs - hi.astype(F32)
        mid = r1.astype(BF16)
        low = (r1 - mid.astype(F32)).astype(BF16)
        mt = mt_ref[...]
        imp = (jnp.dot(mt, hi, preferred_element_type=F32) + jnp.dot(mt, mid, preferred_element_type=F32)
               + jnp.dot(mt, low, preferred_element_type=F32))
        j_io = lax.broadcasted_iota(I32, (n_slc, 1), 0)
        cur = t_row // SLC_BLOCK
        forced = (j_io == 0) | (j_io == cur) | (j_io == cur - 1)
        imp = jnp.where(forced, float("inf"), jnp.where(j_io * SLC_BLOCK <= t_row, imp, NEG_INF))
        sel = jnp.zeros((n_slc, qb), jnp.bool_)
        for _ in range(min(N_SELECT, n_slc)):
            m = jnp.max(imp, axis=0, keepdims=True)
            hit = j_io == jnp.min(jnp.where(imp == m, j_io, n_slc), axis=0, keepdims=True)
            sel = sel | hit
            imp = jnp.where(hit, NEG_INF, imp)
        bias_ref[g] = jnp.where(sel, 0.0, MASK_BIAS)

    q_ts = [jnp.concatenate([qr_t[(g * hpg + hp) * HEAD_DIM:(g * hpg + hp + 1) * HEAD_DIM, :]
                             for hp in range(hpg)], axis=1) for g in range(N_KV_GROUPS)]
    zero_rows = jnp.zeros((128 - HEAD_DIM - blocks_per_tile, hpg * qb), F32)

    def sel_tile(kt, carry, causal):
        k0 = pl.multiple_of(kt * SEL_KEYS, SEL_KEYS)
        b0 = pl.multiple_of(kt * blocks_per_tile, blocks_per_tile)
        out = []
        for g in range(N_KV_GROUPS):
            m_run, acc = carry[g]
            mask_rows = bias_ref[g, pl.ds(b0, blocks_per_tile), :]
            rhs = jnp.concatenate([q_ts[g], jnp.concatenate([mask_rows] * hpg, axis=1), zero_rows], axis=0)
            s = jnp.dot(ks_ref[pl.ds(k0, SEL_KEYS), g * 128:(g + 1) * 128], rhs.astype(BF16),
                        preferred_element_type=F32)
            if causal:
                kpos = k0 + lax.broadcasted_iota(I32, (SEL_KEYS, 1), 0)
                s = jnp.where(kpos <= t_all, s, MASK_BIAS)
            m_new = jnp.maximum(m_run, jnp.max(s, axis=0, keepdims=True))
            p = jnp.exp2(s - m_new)
            pv = jnp.dot(vst_ref[kt, g], p.astype(BF16), preferred_element_type=F32)
            out.append((m_new, jnp.exp2(m_run - m_new) * acc + pv))
        return tuple(out)

    init = tuple((jnp.full((1, hpg * qb), NEG_INF, F32), jnp.zeros((VS_ROWS, hpg * qb), F32))
                 for _ in range(N_KV_GROUPS))
    last = t0 // SEL_KEYS
    carry = lax.fori_loop(0, last, functools.partial(sel_tile, causal=False), init)
    sel_out = sel_tile(last, carry, True)

    heads = []
    for g in range(N_KV_GROUPS):
        lanes = slice(g * HEAD_DIM, (g + 1) * HEAD_DIM)
        q_t = q_ts[g]
        acc = sel_out[g][1]
        l_fin = acc[HEAD_DIM:HEAD_DIM + 1, :]
        o_sel = acc[:HEAD_DIM, :] * (1.0 / jnp.where(l_fin > 0, l_fin, 1.0))

        start = pl.multiple_of(jnp.maximum(t0 - WINDOW, 0), WIN_TILE)
        sw = jnp.dot(kw_ref[pl.ds(start, WIN_KEYS), lanes], q_t.astype(BF16), preferred_element_type=F32)
        dlt = t_all - (start + lax.broadcasted_iota(I32, (WIN_KEYS, 1), 0))
        ew, inv_w = _softmax_keys(sw, (dlt >= 0) & (dlt < WINDOW))
        w0 = start // WIN_TILE
        vwb = jnp.concatenate([vwt_ref[w0 + j, lanes, :] for j in range(WIN_KEYS // WIN_TILE)], axis=1)
        o_win = jnp.dot(vwb, ew.astype(BF16), preferred_element_type=F32) * inv_w

        o_cmp = o_cmp_all[g]
        for hp in range(hpg):
            r = (g * hpg + hp) * N_BRANCH
            cols = slice(hp * qb, (hp + 1) * qb)
            heads.append(gates[r:r + 1] * o_cmp[:, cols] + gates[r + 1:r + 2] * o_sel[:, cols]
                         + gates[r + 2:r + 3] * o_win[:, cols])
    o_ref[...] = jnp.concatenate(heads, axis=0).T.astype(BF16)


def nsa_attention(q, qr, kc, vct, ks, vst, kw, vwt, gates, qb=Q_TILE):
    B, T, nq = q.shape
    n_cmp = kc.shape[1]
    n_slc = T // SLC_BLOCK
    mt = np.zeros((n_slc, n_cmp), np.float32)
    per = SLC_BLOCK // CMP_STRIDE
    for j in range(n_slc):
        for k, wk in enumerate(SLC_OVERLAP_W):
            n = per * j + k - 1
            if 0 <= n < n_cmp - 1:
                mt[j, n] = wk
    per_b = lambda shape: pl.BlockSpec((None,) + shape, lambda b, i: (b,) + (0,) * len(shape))
    return pl.pallas_call(
        functools.partial(_nsa_attn_kernel, qb=qb),
        grid=(B, T // qb),
        in_specs=[pl.BlockSpec((None, qb, nq), lambda b, i: (b, i, 0)),
                  pl.BlockSpec((None, qb, nq), lambda b, i: (b, i, 0)),
                  per_b((n_cmp, KV_WIDTH)), per_b((KV_WIDTH, n_cmp)),
                  per_b((T, KS_WIDTH)), per_b((T // SEL_KEYS, N_KV_GROUPS, VS_ROWS, SEL_KEYS)),
                  per_b((T, KV_WIDTH)), per_b((T // WIN_TILE, KV_WIDTH, WIN_TILE)),
                  pl.BlockSpec((None, N_HEADS * N_BRANCH, qb), lambda b, i: (b, 0, i)),
                  pl.BlockSpec((n_slc, n_cmp), lambda b, i: (0, 0))],
        out_specs=pl.BlockSpec((None, qb, nq), lambda b, i: (b, i, 0)),
        out_shape=jax.ShapeDtypeStruct((B, T, nq), BF16),
        scratch_shapes=[pltpu.VMEM((N_KV_GROUPS, n_slc, qb), F32)],
        compiler_params=_cparams("arbitrary", "arbitrary"),
        name="nsa_attention",
    )(q, qr, kc, vct, ks, vst, kw, vwt, gates, jnp.asarray(mt, BF16))


def _proj_ln_kernel(a_ref, w_ref, x_ref, lng_ref, lnb_ref, o_ref):
    mix = jnp.dot(a_ref[...], w_ref[...], preferred_element_type=F32)
    o_ref[...] = _layer_norm(DEEPNORM_ALPHA * x_ref[...] + mix, lng_ref[...], lnb_ref[...])


def proj_ln(a, w, x, ln_g, ln_b, tq=512):
    N, D = x.shape
    K = a.shape[1]
    row = lambda v: v.reshape(1, -1).astype(F32)
    const = lambda shape: pl.BlockSpec(shape, lambda i: (0,) * len(shape))
    return pl.pallas_call(
        _proj_ln_kernel,
        grid=(N // tq,),
        in_specs=[pl.BlockSpec((tq, K), lambda i: (i, 0)), const((K, D)), pl.BlockSpec((tq, D), lambda i: (i, 0)),
                  const((1, D)), const((1, D))],
        out_specs=pl.BlockSpec((tq, D), lambda i: (i, 0)),
        out_shape=jax.ShapeDtypeStruct((N, D), F32),
        compiler_params=_cparams("arbitrary"),
        name="proj_ln",
    )(a, w.astype(BF16), x, row(ln_g), row(ln_b))


def nsa_layer(x, w_in, cmp_pos_k, cmp_w1_k, cmp_w2_k, cmp_pos_v, cmp_w1_v, cmp_w2_v, w_out, ln_g, ln_b):
    B, T, D = x.shape
    q, qr, kc, vc, ks, kw, vst, vwt, gates = nsa_project(x, w_in)
    kc_cmp = nsa_compress(kc, cmp_pos_k, cmp_w1_k, cmp_w2_k)
    vc_cmp = nsa_compress(vc, cmp_pos_v, cmp_w1_v, cmp_w2_v)
    o = nsa_attention(q, qr, kc_cmp, jnp.swapaxes(vc_cmp, 1, 2), ks, vst, kw, vwt, gates)
    return proj_ln(o.reshape(B * T, -1), w_out, x.reshape(B * T, D), ln_g, ln_b)


def kernel(x, ab_w_in, ab_pool_mix, ab_pool_scale, ab_conv_w, ab_conv_b, ab_conv_ln_g, ab_conv_ln_b, ab_w_out, nsa_w_in, nsa_cmp_pos_k, nsa_cmp_w1_k, nsa_cmp_w2_k, nsa_cmp_pos_v, nsa_cmp_w1_v, nsa_cmp_w2_v, nsa_w_out, ln_mix_g, ln_mix_b, ln_ffn_g, ln_ffn_b, moe_w_router, moe_b_router, moe_w_gate, moe_w_up, moe_w_down, moe_ws_gate, moe_ws_up, moe_ws_down):
    B, T, D = x.shape
    moe = lambda h, i: moe_layer(h, i, moe_w_router[i], moe_b_router[i], moe_w_gate, moe_w_up, moe_w_down,
                                 moe_ws_gate[i], moe_ws_up[i], moe_ws_down[i], ln_ffn_g[i], ln_ffn_b[i])
    h = ab_mixer(x, ab_w_in[0], ab_pool_mix[0], ab_pool_scale[0], ab_conv_w[0], ab_conv_b[0], ab_conv_ln_g[0],
                 ab_conv_ln_b[0], ab_w_out[0], ln_mix_g[0], ln_mix_b[0])
    h = moe(h.reshape(B * T, D), 0)
    h = nsa_layer(h.reshape(B, T, D), nsa_w_in[0], nsa_cmp_pos_k[0], nsa_cmp_w1_k[0], nsa_cmp_w2_k[0],
                  nsa_cmp_pos_v[0], nsa_cmp_w1_v[0], nsa_cmp_w2_v[0], nsa_w_out[0], ln_mix_g[1], ln_mix_b[1])
    h = moe(h, 1)
    return h.reshape(B, T, D)
```

```python
import functools

import jax
import jax.numpy as jnp
import numpy as np
from jax import lax
from jax.experimental import pallas as pl
from jax.experimental.pallas import tpu as pltpu

F32 = jnp.float32
BF16 = jnp.bfloat16
I32 = jnp.int32

D_MODEL = 1024
DEPTH = 2
DEEPNORM_ALPHA = (2.0 * DEPTH) ** 0.25
LN_EPS = 1e-5

POOL_WIDTH = 512
POOL_WINDOWS = (2, 4, 8, 16)
POOL_GROUP = 128
CONV_CH = 512
CONV_TAPS = 31
AB_IN = POOL_WIDTH + 2 * CONV_CH
HALO = 32
CONV_ROWS = 32

N_HEADS = 16
HEAD_DIM = 64
N_KV_GROUPS = 4
HEADS_PER_GROUP = 4
KV_WIDTH = 256
CMP_BLOCK = 32
CMP_STRIDE = 16
CMP_HIDDEN = 128
SLC_BLOCK = 64
N_SELECT = 16
WINDOW = 512
N_BRANCH = 3
ROPE_THETA = 500000.0
ROPE_DIM = 16
SLC_OVERLAP_W = (1.0, 2.0, 2.0, 2.0, 1.0)

N_EXPERTS = 256
TOP_K = 8
N_EXPERT_GROUPS = 8
TOPK_GROUPS = 4
EXPERT_FF = 256
SHARED_FF = 256
ROUTED_SCALE = 2.5
EXPERT_ROWS = 256
EXPERT_BUFFERS = 4
MOE_TQ = 256
SLOT_ROWS = 8
CHUNK_ROWS = 16
EXPERT_BLOCK = 32
TN_DIMS = (((0,), (0,)), ((), ()))

VMEM_LIMIT = 56 * 1024 * 1024
NEG_INF = float("-inf")


def _cparams(*sem):
    return pltpu.CompilerParams(dimension_semantics=sem, vmem_limit_bytes=VMEM_LIMIT)


def _layer_norm(y, g, b):
    mu = jnp.mean(y, axis=-1, keepdims=True)
    d = y - mu
    var = jnp.mean(d * d, axis=-1, keepdims=True)
    return d * lax.rsqrt(var + LN_EPS) * g + b


def _silu(v):
    return v * jax.nn.sigmoid(v)


def _ab_mixer_kernel(x_ref, win_ref, pmix_ref, pscale_ref, cw_ref, cb_ref, clg_ref, clb_ref,
                     wout_ref, lng_ref, lnb_ref, o_ref, eu_ref, ev_ref, cat_ref, *, tq):
    ti = pl.program_id(1)

    @pl.when(ti == 0)
    def _():
        eu_ref[0:HALO, :] = jnp.zeros((HALO, POOL_WIDTH), F32)
        ev_ref[0:HALO, :] = jnp.zeros((HALO, CONV_CH), F32)

    @pl.when(ti > 0)
    def _():
        eu_ref[0:HALO, :] = eu_ref[tq:tq + HALO, :]
        ev_ref[0:HALO, :] = ev_ref[tq:tq + HALO, :]

    x = x_ref[...]
    h = jnp.dot(x.astype(BF16), win_ref[...], preferred_element_type=F32)
    u = h[:, :POOL_WIDTH]
    eu_ref[HALO:HALO + tq, :] = u
    ev_ref[HALO:HALO + tq, :] = h[:, POOL_WIDTH:POOL_WIDTH + CONV_CH] * jax.nn.sigmoid(
        h[:, POOL_WIDTH + CONV_CH:])

    pos1 = ti * tq + lax.broadcasted_iota(I32, (tq, 1), 0) + 1
    for gi, w in enumerate(POOL_WINDOWS):
        lo, hi = gi * POOL_GROUP, (gi + 1) * POOL_GROUP
        ug = u[:, lo:hi]
        s = ug
        for j in range(1, w):
            s = s + eu_ref[HALO - j:HALO - j + tq, lo:hi]
        cnt = jnp.minimum(pos1, w).astype(F32)
        p = s / cnt - ug
        a = jnp.dot(p.astype(BF16), pmix_ref[gi], preferred_element_type=F32)
        cat_ref[:, lo:hi] = (a * pscale_ref[:, lo:hi]).astype(BF16)

    for r in range(tq // CONV_ROWS):
        base = r * CONV_ROWS + HALO - (CONV_TAPS - 1)
        acc = jnp.broadcast_to(cb_ref[...], (CONV_ROWS, CONV_CH))
        for k in range(CONV_TAPS):
            acc = acc + cw_ref[k:k + 1, :] * ev_ref[base + k:base + k + CONV_ROWS, :]
        c = _silu(_layer_norm(acc, clg_ref[...], clb_ref[...]))
        cat_ref[r * CONV_ROWS:(r + 1) * CONV_ROWS, POOL_WIDTH:] = c.astype(BF16)

    mix = jnp.dot(cat_ref[...], wout_ref[...], preferred_element_type=F32)
    o_ref[...] = _layer_norm(DEEPNORM_ALPHA * x + mix, lng_ref[...], lnb_ref[...])


def ab_mixer(x, w_in, pool_mix, pool_scale, conv_w, conv_b, conv_ln_g, conv_ln_b, w_out, ln_g, ln_b,
             tq=256):
    B, T, D = x.shape
    cw = jnp.pad(conv_w.reshape(CONV_TAPS, CONV_CH), ((0, 32 - CONV_TAPS), (0, 0)))
    row = lambda v: v.reshape(1, -1).astype(F32)
    const = lambda shape: pl.BlockSpec(shape, lambda b, t: (0,) * len(shape))
    return pl.pallas_call(
        functools.partial(_ab_mixer_kernel, tq=tq),
        grid=(B, T // tq),
        in_specs=[
            pl.BlockSpec((None, tq, D), lambda b, t: (b, t, 0)),
            const((D, AB_IN)), const((4, POOL_GROUP, POOL_GROUP)), const((1, POOL_WIDTH)),
            const((32, CONV_CH)), const((1, CONV_CH)), const((1, CONV_CH)), const((1, CONV_CH)),
            const((D, D)), const((1, D)), const((1, D)),
        ],
        out_specs=pl.BlockSpec((None, tq, D), lambda b, t: (b, t, 0)),
        out_shape=jax.ShapeDtypeStruct((B, T, D), F32),
        scratch_shapes=[pltpu.VMEM((HALO + tq, POOL_WIDTH), F32),
                        pltpu.VMEM((HALO + tq, CONV_CH), F32),
                        pltpu.VMEM((tq, D), BF16)],
        compiler_params=_cparams("arbitrary", "arbitrary"),
        name="ab_mixer",
    )(x, w_in.astype(BF16), pool_mix.astype(BF16), row(pool_scale), cw, row(conv_b),
      row(conv_ln_g), row(conv_ln_b), w_out.astype(BF16), row(ln_g), row(ln_b))


def _first_index(hit_src, m, iota, size):
    return jnp.min(jnp.where(hit_src == m, iota, size), axis=0, keepdims=True)


def _router_kernel(x_ref, wh_ref, wl_ref, b_ref, tri_ref, idx_ref, wt_ref, lrank_ref, cnt_ref, *, tq):
    x = x_ref[...]
    xh = x.astype(BF16)
    xl = (x - xh.astype(F32)).astype(BF16)
    nt = (((1,), (1,)), ((), ()))
    wh, wl = wh_ref[...], wl_ref[...]
    logits = (lax.dot_general(wh, xh, nt, preferred_element_type=F32)
              + lax.dot_general(wh, xl, nt, preferred_element_type=F32)
              + lax.dot_general(wl, xh, nt, preferred_element_type=F32))
    scores = jax.nn.sigmoid(logits)
    biased = scores + b_ref[...]

    per_group = N_EXPERTS // N_EXPERT_GROUPS
    io_g = lax.broadcasted_iota(I32, (per_group, tq), 0)
    gs_rows = []
    for g in range(N_EXPERT_GROUPS):
        sg = biased[g * per_group:(g + 1) * per_group, :]
        m1 = jnp.max(sg, axis=0, keepdims=True)
        i1 = _first_index(sg, m1, io_g, per_group)
        m2 = jnp.max(jnp.where(io_g == i1, NEG_INF, sg), axis=0, keepdims=True)
        gs_rows.append(m1 + m2)
    gs = jnp.concatenate(gs_rows, axis=0)

    io_ng = lax.broadcasted_iota(I32, (N_EXPERT_GROUPS, tq), 0)
    grp_sel = jnp.zeros((N_EXPERT_GROUPS, tq), jnp.bool_)
    for _ in range(TOPK_GROUPS):
        m = jnp.max(gs, axis=0, keepdims=True)
        hit = io_ng == _first_index(gs, m, io_ng, N_EXPERT_GROUPS)
        grp_sel = grp_sel | hit
        gs = jnp.where(hit, NEG_INF, gs)
    exp_mask = jnp.concatenate(
        [jnp.broadcast_to(grp_sel[g:g + 1, :], (per_group, tq)) for g in range(N_EXPERT_GROUPS)], axis=0)

    masked = jnp.where(exp_mask, biased, NEG_INF)
    io_e = lax.broadcasted_iota(I32, (N_EXPERTS, tq), 0)
    idx_rows, w_rows = [], []
    onehot = jnp.zeros((N_EXPERTS, tq), F32)
    for _ in range(TOP_K):
        m = jnp.max(masked, axis=0, keepdims=True)
        first = _first_index(masked, m, io_e, N_EXPERTS)
        hit = io_e == first
        idx_rows.append(first)
        w_rows.append(jnp.sum(jnp.where(hit, scores, 0.0), axis=0, keepdims=True))
        onehot = onehot + hit.astype(F32)
        masked = jnp.where(hit, NEG_INF, masked)
    w = jnp.concatenate(w_rows, axis=0)
    wt_ref[...] = w / jnp.sum(w, axis=0, keepdims=True) * ROUTED_SCALE
    idx_ref[...] = jnp.concatenate(idx_rows, axis=0)

    oh = onehot.astype(BF16)
    before = jnp.dot(oh, tri_ref[...], preferred_element_type=F32)
    lrank_ref[...] = jnp.concatenate(
        [jnp.sum(jnp.where(io_e == r, before, 0.0), axis=0, keepdims=True) for r in idx_rows],
        axis=0).astype(I32)
    cnt_ref[...] = lax.dot_general(jnp.ones((8, tq), BF16), oh, nt, preferred_element_type=F32).astype(I32)


def moe_router(x, w_router, b_router, tq):
    N, D = x.shape
    wt = w_router.T
    wh = wt.astype(BF16)
    wl = (wt - wh.astype(F32)).astype(BF16)
    tri = (np.arange(tq)[:, None] < np.arange(tq)[None, :]).astype(np.float32)
    const = lambda shape: pl.BlockSpec(shape, lambda i: (0,) * len(shape))
    tok = lambda rows: pl.BlockSpec((rows, tq), lambda i: (0, i))
    return pl.pallas_call(
        functools.partial(_router_kernel, tq=tq),
        grid=(N // tq,),
        in_specs=[pl.BlockSpec((tq, D), lambda i: (i, 0)), const((N_EXPERTS, D)), const((N_EXPERTS, D)),
                  const((N_EXPERTS, 1)), const((tq, tq))],
        out_specs=[tok(TOP_K), tok(TOP_K), tok(TOP_K), pl.BlockSpec((None, 8, N_EXPERTS), lambda i: (i, 0, 0))],
        out_shape=[jax.ShapeDtypeStruct((TOP_K, N), I32), jax.ShapeDtypeStruct((TOP_K, N), F32),
                   jax.ShapeDtypeStruct((TOP_K, N), I32), jax.ShapeDtypeStruct((N // tq, 8, N_EXPERTS), I32)],
        compiler_params=_cparams("arbitrary"),
        name="moe_router",
    )(x, wh, wl, b_router.reshape(N_EXPERTS, 1).astype(F32), jnp.asarray(tri, BF16))


def _pack_pairs(v):
    half = v.shape[1] // 2
    hi = lax.bitcast_convert_type(v[:, :half], I32)
    lo = lax.bitcast_convert_type(v[:, half:], I32)
    return (hi & jnp.int32(-65536)) | lax.shift_right_logical(lo, jnp.int32(16))


def _unpack_pairs(p):
    hi = lax.bitcast_convert_type(p & jnp.int32(-65536), F32)
    lo = lax.bitcast_convert_type(lax.shift_left(p, jnp.int32(16)), F32)
    return jnp.concatenate([hi, lo], axis=1).astype(BF16)


def _round_bf16(v):
    return v.astype(BF16).astype(F32)


def _slot_copies(cnt_ref, off_ref, hbm, buf, sem, ovf_ref, to_hbm):
    def issue(e, carry):
        n_small, n_big, n_ovf = carry
        c = cnt_ref[0, 0, e]
        off = pl.multiple_of(off_ref[0, 0, e], SLOT_ROWS)
        small = (c > 0) & (c <= SLOT_ROWS)
        big = c > SLOT_ROWS
        over = c > CHUNK_ROWS
        base = pl.multiple_of(e * CHUNK_ROWS, CHUNK_ROWS)

        @pl.when(over)
        def _():
            ovf_ref[n_ovf] = e

        @pl.when(small)
        def _():
            pair = (buf.at[pl.ds(base, SLOT_ROWS), :], hbm.at[pl.ds(off, SLOT_ROWS), :])
            pltpu.make_async_copy(*(pair if to_hbm else pair[::-1]), sem).start()

        @pl.when(big)
        def _():
            pair = (buf.at[pl.ds(base, CHUNK_ROWS), :], hbm.at[pl.ds(off, CHUNK_ROWS), :])
            pltpu.make_async_copy(*(pair if to_hbm else pair[::-1]), sem).start()
        return n_small + small.astype(I32), n_big + big.astype(I32), n_ovf + over.astype(I32)
    return lax.fori_loop(0, N_EXPERTS, issue, (0, 0, 0))


def _slot_waits(n_small, n_big, hbm, buf, sem, to_hbm):
    def drain(rows):
        def body(_, c):
            pair = (buf.at[pl.ds(0, rows), :], hbm.at[pl.ds(0, rows), :])
            pltpu.make_async_copy(*(pair if to_hbm else pair[::-1]), sem).wait()
            return c
        return body
    lax.fori_loop(0, n_small, drain(SLOT_ROWS), 0)
    lax.fori_loop(0, n_big, drain(CHUNK_ROWS), 0)


def _slot_rows(lrk_at, fill, tq):
    io_r = lax.broadcasted_iota(I32, (CHUNK_ROWS, tq), 0)
    blocks = []
    for eb in range(N_EXPERTS // EXPERT_BLOCK):
        rows = []
        for e in range(eb * EXPERT_BLOCK, (eb + 1) * EXPERT_BLOCK):
            hit = jnp.broadcast_to(lrk_at[e:e + 1, :], (CHUNK_ROWS, tq)) == io_r
            rows.append(fill(e, hit))
        blocks.append(jnp.concatenate(rows, axis=0))
    return blocks


def _dispatch_kernel(cnt_ref, off_ref, x_ref, idx_ref, lrk_ref, xs_hbm, xbuf, obuf, ovf_ref, pend_ref, sem,
                     osem, *, tq):
    i = pl.program_id(0)
    buf = xbuf.at[i % 2]

    @pl.when(i == 0)
    def _():
        pend_ref[0] = 0
        pend_ref[1] = 0

    xb = x_ref[...].astype(BF16)
    io_e = lax.broadcasted_iota(I32, (N_EXPERTS, tq), 0)
    lrk_at = jnp.full((N_EXPERTS, tq), -1, I32)
    for k in range(TOP_K):
        lrk_at = jnp.where(io_e == idx_ref[k:k + 1, :], lrk_ref[k:k + 1, :], lrk_at)
    rows_per_block = EXPERT_BLOCK * CHUNK_ROWS
    for eb, pick in enumerate(_slot_rows(lrk_at, lambda e, hit: hit.astype(BF16), tq)):
        buf[eb * rows_per_block:(eb + 1) * rows_per_block, :] = _pack_pairs(
            jnp.dot(pick, xb, preferred_element_type=F32))
    _slot_waits(pend_ref[0], pend_ref[1], xs_hbm, buf, sem, True)
    n_small, n_big, n_ovf = _slot_copies(cnt_ref, off_ref, xs_hbm, buf, sem, ovf_ref, True)
    last = i == pl.num_programs(0) - 1
    pend_ref[0] = jnp.where(last, 0, n_small)
    pend_ref[1] = jnp.where(last, 0, n_big)

    @pl.when(last)
    def _():
        _slot_waits(n_small, n_big, xs_hbm, buf, sem, True)

    io_r = lax.broadcasted_iota(I32, (SLOT_ROWS, tq), 0)

    def overflow(i, c):
        e = ovf_ref[i]
        n_units = (cnt_ref[0, 0, e] + SLOT_ROWS - 1) // SLOT_ROWS
        off = pl.multiple_of(off_ref[0, 0, e], SLOT_ROWS)

        def extra(j, c2):
            rank_e = jnp.full((1, tq), -1, I32)
            for k in range(TOP_K):
                rank_e = jnp.where(idx_ref[k:k + 1, :] == e, lrk_ref[k:k + 1, :], rank_e)
            pick = (io_r + j * SLOT_ROWS == rank_e).astype(BF16)
            obuf[...] = _pack_pairs(jnp.dot(pick, xb, preferred_element_type=F32))
            cp = pltpu.make_async_copy(obuf, xs_hbm.at[pl.ds(off + j * SLOT_ROWS, SLOT_ROWS), :], osem)
            cp.start()
            cp.wait()
            return c2
        return lax.fori_loop(CHUNK_ROWS // SLOT_ROWS, n_units, extra, c)
    lax.fori_loop(0, n_ovf, overflow, 0)


def moe_dispatch(x, idx, lrank, tile_cnt, tile_off, n_rows, tq):
    N, D = x.shape
    n_tiles = N // tq
    smem = lambda: pl.BlockSpec((1, 1, N_EXPERTS), lambda i: (i, 0, 0), memory_space=pltpu.SMEM)
    return pl.pallas_call(
        functools.partial(_dispatch_kernel, tq=tq),
        grid=(n_tiles,),
        in_specs=[smem(), smem(), pl.BlockSpec((tq, D), lambda i: (i, 0)),
                  pl.BlockSpec((TOP_K, tq), lambda i: (0, i)), pl.BlockSpec((TOP_K, tq), lambda i: (0, i))],
        out_specs=pl.BlockSpec(memory_space=pl.ANY),
        out_shape=jax.ShapeDtypeStruct((n_rows, D // 2), I32),
        scratch_shapes=[pltpu.VMEM((2, N_EXPERTS * CHUNK_ROWS, D // 2), I32), pltpu.VMEM((SLOT_ROWS, D // 2), I32),
                        pltpu.SMEM((N_EXPERTS,), I32), pltpu.SMEM((2,), I32), pltpu.SemaphoreType.DMA(()),
                        pltpu.SemaphoreType.DMA(())],
        compiler_params=_cparams("arbitrary"),
        name="moe_dispatch",
    )(tile_cnt.reshape(n_tiles, 1, N_EXPERTS), tile_off.reshape(n_tiles, 1, N_EXPERTS), x, idx, lrank)


def _expert_kernel(row0_ref, nblk_ref, used_ref, x_hbm, wg_ref, wu_ref, wd_ref, y_hbm, wgb, wub, wdb, xin, yout,
                   in_sem, out_sem):
    e = pl.program_id(0)
    m = EXPERT_ROWS
    nb = nblk_ref[e]
    row0 = row0_ref[e]
    rows_of = lambda j: pl.ds(pl.multiple_of(row0 + j * m, m), m)
    in_copy = lambda j, slot: pltpu.make_async_copy(x_hbm.at[rows_of(j), :], xin.at[slot], in_sem.at[slot])
    out_copy = lambda j, slot: pltpu.make_async_copy(yout.at[slot], y_hbm.at[rows_of(j), :], out_sem.at[slot])

    @pl.when(nb > 0)
    def _():
        for j0 in range(EXPERT_BUFFERS - 1):
            @pl.when(j0 < nb)
            def _(j0=j0):
                in_copy(j0, j0).start()
        wgb[...] = wg_ref[...].astype(BF16)
        wub[...] = wu_ref[...].astype(BF16)
        wdb[...] = wd_ref[...].astype(BF16)

        def block(j, c):
            slot = j % EXPERT_BUFFERS
            in_copy(j, slot).wait()
            ahead = j + EXPERT_BUFFERS - 1

            @pl.when(ahead < nb)
            def _():
                in_copy(ahead, ahead % EXPERT_BUFFERS).start()

            @pl.when(j >= EXPERT_BUFFERS)
            def _():
                out_copy(j - EXPERT_BUFFERS, slot).wait()

            rows = lax.broadcasted_iota(I32, (m, xin.shape[2]), 0)
            xb = _unpack_pairs(jnp.where(rows < used_ref[e] - j * m, xin[slot], 0))
            g = jnp.dot(xb, wgb[...], preferred_element_type=F32)
            u = jnp.dot(xb, wub[...], preferred_element_type=F32)
            hid = (_silu(g) * u).astype(BF16)
            yout[slot] = _pack_pairs(_round_bf16(jnp.dot(hid, wdb[...], preferred_element_type=F32)))
            out_copy(j, slot).start()
            return c
        lax.fori_loop(0, nb, block, 0)

        for back in range(1, EXPERT_BUFFERS + 1):
            @pl.when(nb >= back)
            def _(back=back):
                out_copy(nb - back, (nb - back) % EXPERT_BUFFERS).wait()


def moe_experts(xs, row0, nblk, used, w_gate, w_up, w_down, layer):
    m = EXPERT_ROWS
    D = xs.shape[1] * 2
    wspec = lambda r, c: pl.BlockSpec((None, None, r, c), lambda e, *_: (layer, e, 0, 0))
    grid_spec = pltpu.PrefetchScalarGridSpec(
        num_scalar_prefetch=3,
        grid=(N_EXPERTS,),
        in_specs=[pl.BlockSpec(memory_space=pl.ANY), wspec(D, EXPERT_FF), wspec(D, EXPERT_FF), wspec(EXPERT_FF, D)],
        out_specs=pl.BlockSpec(memory_space=pl.ANY),
        scratch_shapes=[pltpu.VMEM((D, EXPERT_FF), BF16), pltpu.VMEM((D, EXPERT_FF), BF16),
                        pltpu.VMEM((EXPERT_FF, D), BF16), pltpu.VMEM((EXPERT_BUFFERS, m, D // 2), I32),
                        pltpu.VMEM((EXPERT_BUFFERS, m, D // 2), I32), pltpu.SemaphoreType.DMA((EXPERT_BUFFERS,)),
                        pltpu.SemaphoreType.DMA((EXPERT_BUFFERS,))],
    )
    return pl.pallas_call(
        _expert_kernel,
        grid_spec=grid_spec,
        out_shape=jax.ShapeDtypeStruct(xs.shape, I32),
        compiler_params=_cparams("arbitrary"),
        name="moe_experts",
    )(row0, nblk, used, xs, w_gate, w_up, w_down)


def _combine_kernel(cnt_ref, off_ref, y_hbm, idx_ref, lrk_ref, w_ref, x_ref, wsg_ref, wsu_ref, wsd_ref,
                    lng_ref, lnb_ref, o_ref, ybuf, obuf, acc_ref, ovf_ref, sem, *, tq):
    @pl.when(pl.program_id(0) == 0)
    def _():
        ybuf[...] = jnp.zeros_like(ybuf)

    n_small, n_big, n_ovf = _slot_copies(cnt_ref, off_ref, y_hbm, ybuf, sem, ovf_ref, False)

    x = x_ref[...]
    xb = x.astype(BF16)
    g = jnp.dot(xb, wsg_ref[...], preferred_element_type=F32)
    u = jnp.dot(xb, wsu_ref[...], preferred_element_type=F32)
    acc_ref[...] = jnp.dot((_silu(g) * u).astype(BF16), wsd_ref[...], preferred_element_type=F32)

    io_e = lax.broadcasted_iota(I32, (N_EXPERTS, tq), 0)
    w_at = jnp.zeros((N_EXPERTS, tq), F32)
    lrk_at = jnp.full((N_EXPERTS, tq), -1, I32)
    for k in range(TOP_K):
        hit = io_e == idx_ref[k:k + 1, :]
        w_at = jnp.where(hit, w_ref[k:k + 1, :], w_at)
        lrk_at = jnp.where(hit, lrk_ref[k:k + 1, :], lrk_at)
    fill = lambda e, hit: jnp.where(hit, jnp.broadcast_to(w_at[e:e + 1, :], hit.shape), 0.0).astype(BF16)
    weights = _slot_rows(lrk_at, fill, tq)

    _slot_waits(n_small, n_big, y_hbm, ybuf, sem, False)

    rows_per_block = EXPERT_BLOCK * CHUNK_ROWS
    for eb, wct in enumerate(weights):
        yb = _unpack_pairs(ybuf[eb * rows_per_block:(eb + 1) * rows_per_block, :])
        acc_ref[...] += lax.dot_general(wct, yb, TN_DIMS, preferred_element_type=F32)

    io_r = lax.broadcasted_iota(I32, (SLOT_ROWS, tq), 0)

    def overflow(i, c):
        e = ovf_ref[i]
        n_units = (cnt_ref[0, 0, e] + SLOT_ROWS - 1) // SLOT_ROWS
        off = pl.multiple_of(off_ref[0, 0, e], SLOT_ROWS)

        def extra(j, c2):
            cp = pltpu.make_async_copy(y_hbm.at[pl.ds(off + j * SLOT_ROWS, SLOT_ROWS), :], obuf, sem)
            cp.start()
            w_e = jnp.zeros((1, tq), F32)
            rank_e = jnp.full((1, tq), -1, I32)
            for k in range(TOP_K):
                hit = idx_ref[k:k + 1, :] == e
                w_e = jnp.where(hit, w_ref[k:k + 1, :], w_e)
                rank_e = jnp.where(hit, lrk_ref[k:k + 1, :], rank_e)
            wct = jnp.where(io_r + j * SLOT_ROWS == rank_e, w_e, 0.0).astype(BF16)
            cp.wait()
            acc_ref[...] += lax.dot_general(wct, _unpack_pairs(obuf[...]), TN_DIMS, preferred_element_type=F32)
            return c2
        return lax.fori_loop(CHUNK_ROWS // SLOT_ROWS, n_units, extra, c)
    lax.fori_loop(0, n_ovf, overflow, 0)

    o_ref[...] = _layer_norm(DEEPNORM_ALPHA * x + acc_ref[...], lng_ref[...], lnb_ref[...])


def moe_combine(x, ys, idx, lrank, wts, tile_cnt, tile_off, ws_gate, ws_up, ws_down, ln_g, ln_b, tq):
    N, D = x.shape
    n_tiles = N // tq
    row = lambda v: v.reshape(1, -1).astype(F32)
    const = lambda shape: pl.BlockSpec(shape, lambda i: (0,) * len(shape))
    smem = lambda: pl.BlockSpec((1, 1, N_EXPERTS), lambda i: (i, 0, 0), memory_space=pltpu.SMEM)
    tokl = lambda: pl.BlockSpec((TOP_K, tq), lambda i: (0, i))
    return pl.pallas_call(
        functools.partial(_combine_kernel, tq=tq),
        grid=(n_tiles,),
        in_specs=[smem(), smem(), pl.BlockSpec(memory_space=pl.ANY), tokl(), tokl(), tokl(),
                  pl.BlockSpec((tq, D), lambda i: (i, 0)),
                  const((D, SHARED_FF)), const((D, SHARED_FF)), const((SHARED_FF, D)), const((1, D)), const((1, D))],
        out_specs=pl.BlockSpec((tq, D), lambda i: (i, 0)),
        out_shape=jax.ShapeDtypeStruct((N, D), F32),
        scratch_shapes=[pltpu.VMEM((N_EXPERTS * CHUNK_ROWS, D // 2), I32), pltpu.VMEM((SLOT_ROWS, D // 2), I32),
                        pltpu.VMEM((tq, D), F32), pltpu.SMEM((N_EXPERTS,), I32), pltpu.SemaphoreType.DMA(())],
        compiler_params=_cparams("arbitrary"),
        name="moe_combine",
    )(tile_cnt.reshape(n_tiles, 1, N_EXPERTS), tile_off.reshape(n_tiles, 1, N_EXPERTS), ys, idx, lrank,
      wts, x, ws_gate.astype(BF16), ws_up.astype(BF16), ws_down.astype(BF16), row(ln_g), row(ln_b))


def moe_layer(x, layer, w_router, b_router, w_gate, w_up, w_down, ws_gate, ws_up, ws_down, ln_g, ln_b):
    N, D = x.shape
    m = EXPERT_ROWS
    tq = MOE_TQ
    idx, wts, lrank, tile_cnt = moe_router(x, w_router, b_router, tq)
    tile_cnt = tile_cnt[:, 0, :]
    slots = (tile_cnt + SLOT_ROWS - 1) // SLOT_ROWS * SLOT_ROWS
    used = jnp.sum(slots, axis=0)
    padded = (used + m - 1) // m * m
    pend = jnp.cumsum(padded)
    pstart = pend - padded
    tile_off = pstart[None, :] + jnp.cumsum(slots, axis=0) - slots
    n_assign = N * TOP_K
    max_rows = n_assign + (SLOT_ROWS - 1) * min(n_assign, (N // tq) * N_EXPERTS) + N_EXPERTS * (m - 1)
    xs = moe_dispatch(x, idx, lrank, tile_cnt, tile_off, -(-max_rows // m) * m, tq)
    ys = moe_experts(xs, pstart.astype(I32), (padded // m).astype(I32), used.astype(I32), w_gate, w_up, w_down, layer)
    return moe_combine(x, ys, idx, lrank, wts, tile_cnt, tile_off, ws_gate, ws_up, ws_down, ln_g, ln_b, tq)


Q_TILE = 128
SEL_KEYS = 512
WIN_TILE = 128
WIN_KEYS = WINDOW + Q_TILE
PROJ_TQ = SEL_KEYS
KS_WIDTH = N_KV_GROUPS * 128
VS_ROWS = HEAD_DIM + 16
NSA_TOK_COLS = N_HEADS * HEAD_DIM + 3 * KV_WIDTH + KS_WIDTH
NSA_FEAT_ROWS = 2 * KV_WIDTH + N_HEADS * N_BRANCH
NT_DIMS = (((1,), (1,)), ((), ()))
LOG2E = 1.4426950408889634
MASK_BIAS = -1e30


def _rotary(v, cos, sin, lo_mask):
    parts = []
    for c in range(v.shape[1] // 128):
        blk = v[:, c * 128:(c + 1) * 128]
        nxt = pltpu.roll(blk, 128 - ROPE_DIM // 2, axis=1)
        prv = pltpu.roll(blk, ROPE_DIM // 2, axis=1)
        parts.append(blk * cos + jnp.where(lo_mask, nxt, prv) * sin)
    return jnp.concatenate(parts, axis=1)


def _nsa_proj_kernel(x_ref, wtok_ref, wfeat_ref, cos_ref, sin_ref, cosk_ref, sink_ref, q_ref, qr_ref, kc_ref,
                     vc_ref, ks_ref, kw_ref, vs_ref, vw_ref, g_ref, *, tq):
    xb = x_ref[...].astype(BF16)
    h = jnp.dot(xb, wtok_ref[...], preferred_element_type=F32)
    cos, sin = cos_ref[...], sin_ref[...]
    lane = lax.broadcasted_iota(I32, (tq, 128), 1)
    lo_mask = (lane % HEAD_DIM) < ROPE_DIM // 2
    nq = N_HEADS * HEAD_DIM
    q = h[:, :nq] * (HEAD_DIM ** -0.5 * LOG2E)
    q_ref[...] = q.astype(BF16)
    qr_ref[...] = _rotary(q, cos, sin, lo_mask).astype(BF16)
    c0 = nq
    kc_ref[...] = h[:, c0:c0 + KV_WIDTH].astype(BF16)
    vc_ref[...] = h[:, c0 + KV_WIDTH:c0 + 2 * KV_WIDTH].astype(BF16)
    c0 += 2 * KV_WIDTH
    ks = _rotary(h[:, c0:c0 + N_KV_GROUPS * 128], cosk_ref[...], sink_ref[...], lo_mask)
    blk = lax.broadcasted_iota(I32, (tq, 128), 0) % SEL_KEYS // SLC_BLOCK
    onehot = (lane - HEAD_DIM == blk).astype(F32)
    ks_ref[...] = (ks + jnp.concatenate([onehot] * N_KV_GROUPS, axis=1)).astype(BF16)
    c0 += N_KV_GROUPS * 128
    kw_ref[...] = _rotary(h[:, c0:c0 + KV_WIDTH], cos, sin, lo_mask).astype(BF16)
    hf = lax.dot_general(wfeat_ref[...], xb, NT_DIMS, preferred_element_type=F32)
    ones_row = (lax.broadcasted_iota(I32, (VS_ROWS - HEAD_DIM, tq), 0) == 0).astype(BF16)
    for g in range(N_KV_GROUPS):
        vs_ref[g, 0:HEAD_DIM, :] = hf[g * HEAD_DIM:(g + 1) * HEAD_DIM].astype(BF16)
        vs_ref[g, HEAD_DIM:VS_ROWS, :] = ones_row
    for j in range(tq // WIN_TILE):
        vw_ref[j] = hf[KV_WIDTH:2 * KV_WIDTH, j * WIN_TILE:(j + 1) * WIN_TILE].astype(BF16)
    g_ref[...] = jax.nn.sigmoid(hf[2 * KV_WIDTH:])


def nsa_project(x, w_in, tq=PROJ_TQ):
    B, T, D = x.shape
    nq = N_HEADS * HEAD_DIM
    q, kc, vc, ks, vs, kw, vw, g = jnp.split(w_in, [nq + i * KV_WIDTH for i in range(7)], axis=1)
    ks_wide = jnp.pad(ks.reshape(D, N_KV_GROUPS, HEAD_DIM), ((0, 0), (0, 0), (0, 128 - HEAD_DIM)))
    wtok = jnp.concatenate([q, kc, vc, ks_wide.reshape(D, KS_WIDTH), kw], axis=1).astype(BF16)
    wfeat = jnp.concatenate([vs, vw, g], axis=1).T.astype(BF16)
    inv_freq = jnp.power(jnp.float32(ROPE_THETA), -jnp.arange(0, ROPE_DIM, 2, dtype=F32) / ROPE_DIM)
    ang = jnp.arange(T).astype(F32)[:, None] * inv_freq[None, :]
    ones = jnp.ones((T, HEAD_DIM - ROPE_DIM), F32)
    cos_h = jnp.concatenate([jnp.cos(ang), jnp.cos(ang), ones], axis=1)
    sin_h = jnp.concatenate([-jnp.sin(ang), jnp.sin(ang), 0.0 * ones], axis=1)
    ident = jnp.ones((T, 128 - HEAD_DIM), F32)
    tables = [jnp.concatenate([cos_h, cos_h], axis=1), jnp.concatenate([sin_h, sin_h], axis=1),
              jnp.concatenate([cos_h, ident], axis=1), jnp.concatenate([sin_h, 0.0 * ident], axis=1)]
    nt = T // tq
    tokm = lambda w: pl.BlockSpec((None, tq, w), lambda b, t: (b, t, 0))
    const = lambda shape: pl.BlockSpec(shape, lambda b, t: (0,) * len(shape))
    table = lambda: pl.BlockSpec((tq, 128), lambda b, t: (t, 0))
    bf = lambda *shape: jax.ShapeDtypeStruct(shape, BF16)
    return pl.pallas_call(
        functools.partial(_nsa_proj_kernel, tq=tq),
        grid=(B, nt),
        in_specs=[tokm(D), const((D, NSA_TOK_COLS)), const((NSA_FEAT_ROWS, D)), table(), table(), table(), table()],
        out_specs=[tokm(nq), tokm(nq), tokm(KV_WIDTH), tokm(KV_WIDTH), tokm(KS_WIDTH), tokm(KV_WIDTH),
                   pl.BlockSpec((None, None, N_KV_GROUPS, VS_ROWS, tq), lambda b, t: (b, t, 0, 0, 0)),
                   pl.BlockSpec((None, tq // WIN_TILE, KV_WIDTH, WIN_TILE), lambda b, t: (b, t, 0, 0)),
                   pl.BlockSpec((None, N_HEADS * N_BRANCH, tq), lambda b, t: (b, 0, t))],
        out_shape=[bf(B, T, nq), bf(B, T, nq), bf(B, T, KV_WIDTH), bf(B, T, KV_WIDTH), bf(B, T, KS_WIDTH),
                   bf(B, T, KV_WIDTH), bf(B, T // tq, N_KV_GROUPS, VS_ROWS, tq),
                   bf(B, T // WIN_TILE, KV_WIDTH, WIN_TILE),
                   jax.ShapeDtypeStruct((B, N_HEADS * N_BRANCH, T), F32)],
        compiler_params=_cparams("arbitrary", "arbitrary"),
        name="nsa_project",
    )(x, wtok, wfeat, *tables)


def _compress_kernel(x_ref, pos_ref, w1_ref, w2_ref, o_ref):
    blocks = (x_ref[...].astype(F32) + pos_ref[...]).astype(BF16)
    pre = jnp.dot(blocks, w1_ref[...], preferred_element_type=F32)
    hid = jax.nn.gelu(pre, approximate=True)
    o_ref[...] = jnp.dot(hid.astype(BF16), w2_ref[...], preferred_element_type=F32).astype(BF16)


def nsa_compress(kv, pos_emb, w1, w2, tr=512):
    B, T, _ = kv.shape
    nch = T // CMP_STRIDE
    ch = kv.reshape(B, nch, CMP_STRIDE, N_KV_GROUPS, HEAD_DIM)
    blocks = jnp.concatenate([ch[:, :-1], ch[:, 1:]], axis=2)
    flat = jnp.moveaxis(blocks, 3, 2).reshape(B, nch - 1, N_KV_GROUPS, CMP_BLOCK * HEAD_DIM)
    flat = jnp.pad(flat, ((0, 0), (0, 1), (0, 0), (0, 0))).reshape(B * nch * N_KV_GROUPS, CMP_BLOCK * HEAD_DIM)
    rows = flat.shape[0]
    tr = min(tr, rows)
    const = lambda shape: pl.BlockSpec(shape, lambda i: (0,) * len(shape))
    out = pl.pallas_call(
        _compress_kernel,
        grid=(rows // tr,),
        in_specs=[pl.BlockSpec((tr, CMP_BLOCK * HEAD_DIM), lambda i: (i, 0)), const((1, CMP_BLOCK * HEAD_DIM)),
                  const((CMP_BLOCK * HEAD_DIM, CMP_HIDDEN)), const((CMP_HIDDEN, HEAD_DIM))],
        out_specs=pl.BlockSpec((tr, HEAD_DIM), lambda i: (i, 0)),
        out_shape=jax.ShapeDtypeStruct((rows, HEAD_DIM), BF16),
        compiler_params=_cparams("arbitrary"),
        name="nsa_compress",
    )(flat, pos_emb.reshape(1, -1).astype(F32), w1.astype(BF16), w2.astype(BF16))
    return out.reshape(B, nch, N_KV_GROUPS * HEAD_DIM)


def _softmax_keys(s, mask):
    s = jnp.where(mask, s, NEG_INF)
    m = jnp.max(s, axis=0, keepdims=True)
    m = jnp.where((m > NEG_INF) & (m < float("inf")), m, 0.0)
    e = jnp.exp2(s - m)
    d = jnp.sum(e, axis=0, keepdims=True)
    return e, 1.0 / jnp.where(d > 0, d, 1.0)


def _nsa_attn_kernel(q_ref, qr_ref, kc_ref, vct_ref, ks_ref, vst_ref, kw_ref, vwt_ref, g_ref, mt_ref, o_ref,
                     bias_ref, *, qb):
    t0 = pl.program_id(1) * qb
    hpg = HEADS_PER_GROUP
    t_row = t0 + lax.broadcasted_iota(I32, (1, qb), 1)
    t_all = jnp.concatenate([t_row] * hpg, axis=1)
    q = q_ref[...]
    qr_t = qr_ref[...].astype(F32).T
    gates = g_ref[...]
    n_cmp = kc_ref.shape[0]
    n_slc = mt_ref.shape[0]
    blocks_per_tile = SEL_KEYS // SLC_BLOCK
    o_cmp_all = []
    for g in range(N_KV_GROUPS):
        lanes = slice(g * HEAD_DIM, (g + 1) * HEAD_DIM)
        qg = jnp.concatenate(
            [q[:, (g * hpg + hp) * HEAD_DIM:(g * hpg + hp + 1) * HEAD_DIM] for hp in range(hpg)], axis=0)

        sc = lax.dot_general(kc_ref[:, lanes], qg, NT_DIMS, preferred_element_type=F32)
        cmp_end = lax.broadcasted_iota(I32, (n_cmp, 1), 0) * CMP_STRIDE + (CMP_BLOCK - 1)
        ec, inv_c = _softmax_keys(sc, cmp_end <= t_all)
        pc = ec * inv_c
        o_cmp_all.append(jnp.dot(vct_ref[lanes, :], pc.astype(BF16), preferred_element_type=F32))

        pcs = pc[:, :qb]
        for hp in range(1, hpg):
            pcs = pcs + pc[:, hp * qb:(hp + 1) * qb]
        hi = pcs.astype(BF16)
        r1 = pcs - hi.astype(F32)
        mid = r1.astype(BF16)
        low = (r1 - mid.astype(F32)).astype(BF16)
        mt = mt_ref[...]
        imp = (jnp.dot(mt, hi, preferred_element_type=F32) + jnp.dot(mt, mid, preferred_element_type=F32)
               + jnp.dot(mt, low, preferred_element_type=F32))
        j_io = lax.broadcasted_iota(I32, (n_slc, 1), 0)
        cur = t_row // SLC_BLOCK
        forced = (j_io == 0) | (j_io == cur) | (j_io == cur - 1)
        imp = jnp.where(forced, float("inf"), jnp.where(j_io * SLC_BLOCK <= t_row, imp, NEG_INF))
        sel = jnp.zeros((n_slc, qb), jnp.bool_)
        for _ in range(min(N_SELECT, n_slc)):
            m = jnp.max(imp, axis=0, keepdims=True)
            hit = j_io == jnp.min(jnp.where(imp == m, j_io, n_slc), axis=0, keepdims=True)
            sel = sel | hit
            imp = jnp.where(hit, NEG_INF, imp)
        bias_ref[g] = jnp.where(sel, 0.0, MASK_BIAS)

    q_ts = [jnp.concatenate([qr_t[(g * hpg + hp) * HEAD_DIM:(g * hpg + hp + 1) * HEAD_DIM, :]
                             for hp in range(hpg)], axis=1) for g in range(N_KV_GROUPS)]
    zero_rows = jnp.zeros((128 - HEAD_DIM - blocks_per_tile, hpg * qb), F32)

    def sel_tile(kt, carry, causal):
        k0 = pl.multiple_of(kt * SEL_KEYS, SEL_KEYS)
        b0 = pl.multiple_of(kt * blocks_per_tile, blocks_per_tile)
        out = []
        for g in range(N_KV_GROUPS):
            m_run, acc = carry[g]
            mask_rows = bias_ref[g, pl.ds(b0, blocks_per_tile), :]
            rhs = jnp.concatenate([q_ts[g], jnp.concatenate([mask_rows] * hpg, axis=1), zero_rows], axis=0)
            s = jnp.dot(ks_ref[pl.ds(k0, SEL_KEYS), g * 128:(g + 1) * 128], rhs.astype(BF16),
                        preferred_element_type=F32)
            if causal:
                kpos = k0 + lax.broadcasted_iota(I32, (SEL_KEYS, 1), 0)
                s = jnp.where(kpos <= t_all, s, MASK_BIAS)
            m_new = jnp.maximum(m_run, jnp.max(s, axis=0, keepdims=True))
            p = jnp.exp2(s - m_new)
            pv = jnp.dot(vst_ref[kt, g], p.astype(BF16), preferred_element_type=F32)
            out.append((m_new, jnp.exp2(m_run - m_new) * acc + pv))
        return tuple(out)

    init = tuple((jnp.full((1, hpg * qb), NEG_INF, F32), jnp.zeros((VS_ROWS, hpg * qb), F32))
                 for _ in range(N_KV_GROUPS))
    last = t0 // SEL_KEYS
    carry = lax.fori_loop(0, last, functools.partial(sel_tile, causal=False), init)
    sel_out = sel_tile(last, carry, True)

    heads = []
    for g in range(N_KV_GROUPS):
        lanes = slice(g * HEAD_DIM, (g + 1) * HEAD_DIM)
        q_t = q_ts[g]
        acc = sel_out[g][1]
        l_fin = acc[HEAD_DIM:HEAD_DIM + 1, :]
        o_sel = acc[:HEAD_DIM, :] * (1.0 / jnp.where(l_fin > 0, l_fin, 1.0))

        start = pl.multiple_of(jnp.maximum(t0 - WINDOW, 0), WIN_TILE)
        sw = jnp.dot(kw_ref[pl.ds(start, WIN_KEYS), lanes], q_t.astype(BF16), preferred_element_type=F32)
        dlt = t_all - (start + lax.broadcasted_iota(I32, (WIN_KEYS, 1), 0))
        ew, inv_w = _softmax_keys(sw, (dlt >= 0) & (dlt < WINDOW))
        w0 = start // WIN_TILE
        vwb = jnp.concatenate([vwt_ref[w0 + j, lanes, :] for j in range(WIN_KEYS // WIN_TILE)], axis=1)
        o_win = jnp.dot(vwb, ew.astype(BF16), preferred_element_type=F32) * inv_w

        o_cmp = o_cmp_all[g]
        for hp in range(hpg):
            r = (g * hpg + hp) * N_BRANCH
            cols = slice(hp * qb, (hp + 1) * qb)
            heads.append(gates[r:r + 1] * o_cmp[:, cols] + gates[r + 1:r + 2] * o_sel[:, cols]
                         + gates[r + 2:r + 3] * o_win[:, cols])
    o_ref[...] = jnp.concatenate(heads, axis=0).T.astype(BF16)


def nsa_attention(q, qr, kc, vct, ks, vst, kw, vwt, gates, qb=Q_TILE):
    B, T, nq = q.shape
    n_cmp = kc.shape[1]
    n_slc = T // SLC_BLOCK
    mt = np.zeros((n_slc, n_cmp), np.float32)
    per = SLC_BLOCK // CMP_STRIDE
    for j in range(n_slc):
        for k, wk in enumerate(SLC_OVERLAP_W):
            n = per * j + k - 1
            if 0 <= n < n_cmp - 1:
                mt[j, n] = wk
    per_b = lambda shape: pl.BlockSpec((None,) + shape, lambda b, i: (b,) + (0,) * len(shape))
    return pl.pallas_call(
        functools.partial(_nsa_attn_kernel, qb=qb),
        grid=(B, T // qb),
        in_specs=[pl.BlockSpec((None, qb, nq), lambda b, i: (b, i, 0)),
                  pl.BlockSpec((None, qb, nq), lambda b, i: (b, i, 0)),
                  per_b((n_cmp, KV_WIDTH)), per_b((KV_WIDTH, n_cmp)),
                  per_b((T, KS_WIDTH)), per_b((T // SEL_KEYS, N_KV_GROUPS, VS_ROWS, SEL_KEYS)),
                  per_b((T, KV_WIDTH)), per_b((T // WIN_TILE, KV_WIDTH, WIN_TILE)),
                  pl.BlockSpec((None, N_HEADS * N_BRANCH, qb), lambda b, i: (b, 0, i)),
                  pl.BlockSpec((n_slc, n_cmp), lambda b, i: (0, 0))],
        out_specs=pl.BlockSpec((None, qb, nq), lambda b, i: (b, i, 0)),
        out_shape=jax.ShapeDtypeStruct((B, T, nq), BF16),
        scratch_shapes=[pltpu.VMEM((N_KV_GROUPS, n_slc, qb), F32)],
        compiler_params=_cparams("arbitrary", "arbitrary"),
        name="nsa_attention",
    )(q, qr, kc, vct, ks, vst, kw, vwt, gates, jnp.asarray(mt, BF16))


def _proj_ln_kernel(a_ref, w_ref, x_ref, lng_ref, lnb_ref, o_ref):
    mix = jnp.dot(a_ref[...], w_ref[...], preferred_element_type=F32)
    o_ref[...] = _layer_norm(DEEPNORM_ALPHA * x_ref[...] + mix, lng_ref[...], lnb_ref[...])


def proj_ln(a, w, x, ln_g, ln_b, tq=512):
    N, D = x.shape
    K = a.shape[1]
    row = lambda v: v.reshape(1, -1).astype(F32)
    const = lambda shape: pl.BlockSpec(shape, lambda i: (0,) * len(shape))
    return pl.pallas_call(
        _proj_ln_kernel,
        grid=(N // tq,),
        in_specs=[pl.BlockSpec((tq, K), lambda i: (i, 0)), const((K, D)), pl.BlockSpec((tq, D), lambda i: (i, 0)),
                  const((1, D)), const((1, D))],
        out_specs=pl.BlockSpec((tq, D), lambda i: (i, 0)),
        out_shape=jax.ShapeDtypeStruct((N, D), F32),
        compiler_params=_cparams("arbitrary"),
        name="proj_ln",
    )(a, w.astype(BF16), x, row(ln_g), row(ln_b))


def nsa_layer(x, w_in, cmp_pos_k, cmp_w1_k, cmp_w2_k, cmp_pos_v, cmp_w1_v, cmp_w2_v, w_out, ln_g, ln_b):
    B, T, D = x.shape
    q, qr, kc, vc, ks, kw, vst, vwt, gates = nsa_project(x, w_in)
    kc_cmp = nsa_compress(kc, cmp_pos_k, cmp_w1_k, cmp_w2_k)
    vc_cmp = nsa_compress(vc, cmp_pos_v, cmp_w1_v, cmp_w2_v)
    o = nsa_attention(q, qr, kc_cmp, jnp.swapaxes(vc_cmp, 1, 2), ks, vst, kw, vwt, gates)
    return proj_ln(o.reshape(B * T, -1), w_out, x.reshape(B * T, D), ln_g, ln_b)


def kernel(x, ab_w_in, ab_pool_mix, ab_pool_scale, ab_conv_w, ab_conv_b, ab_conv_ln_g, ab_conv_ln_b, ab_w_out, nsa_w_in, nsa_cmp_pos_k, nsa_cmp_w1_k, nsa_cmp_w2_k, nsa_cmp_pos_v, nsa_cmp_w1_v, nsa_cmp_w2_v, nsa_w_out, ln_mix_g, ln_mix_b, ln_ffn_g, ln_ffn_b, moe_w_router, moe_b_router, moe_w_gate, moe_w_up, moe_w_down, moe_ws_gate, moe_ws_up, moe_ws_down):
    B, T, D = x.shape
    moe = lambda h, i: moe_layer(h, i, moe_w_router[i], moe_b_router[i], moe_w_gate, moe_w_up, moe_w_down,
                                 moe_ws_gate[i], moe_ws_up[i], moe_ws_down[i], ln_ffn_g[i], ln_ffn_b[i])
    h = ab_mixer(x, ab_w_in[0], ab_pool_mix[0], ab_pool_scale[0], ab_conv_w[0], ab_conv_b[0], ab_conv_ln_g[0],
                 ab_conv_ln_b[0], ab_w_out[0], ln_mix_g[0], ln_mix_b[0])
    h = moe(h.reshape(B * T, D), 0)
    h = nsa_layer(h.reshape(B, T, D), nsa_w_in[0], nsa_cmp_pos_k[0], nsa_cmp_w1_k[0], nsa_cmp_w2_k[0],
                  nsa_cmp_pos_v[0], nsa_cmp_w1_v[0], nsa_cmp_w2_v[0], nsa_w_out[0], ln_mix_g[1], ln_mix_b[1])
    h = moe(h, 1)
    return h.reshape(B, T, D)
```

```python
import functools

import jax
import jax.numpy as jnp
import numpy as np
from jax import lax
from jax.experimental import pallas as pl
from jax.experimental.pallas import tpu as pltpu

F32 = jnp.float32
BF16 = jnp.bfloat16
I32 = jnp.int32

D_MODEL = 1024
DEPTH = 2
DEEPNORM_ALPHA = (2.0 * DEPTH) ** 0.25
LN_EPS = 1e-5

POOL_WIDTH = 512
POOL_WINDOWS = (2, 4, 8, 16)
POOL_GROUP = 128
CONV_CH = 512
CONV_TAPS = 31
AB_IN = POOL_WIDTH + 2 * CONV_CH
HALO = 32
CONV_ROWS = 32

N_HEADS = 16
HEAD_DIM = 64
N_KV_GROUPS = 4
HEADS_PER_GROUP = 4
KV_WIDTH = 256
CMP_BLOCK = 32
CMP_STRIDE = 16
CMP_HIDDEN = 128
SLC_BLOCK = 64
N_SELECT = 16
WINDOW = 512
N_BRANCH = 3
ROPE_THETA = 500000.0
ROPE_DIM = 16
SLC_OVERLAP_W = (1.0, 2.0, 2.0, 2.0, 1.0)

N_EXPERTS = 256
TOP_K = 8
N_EXPERT_GROUPS = 8
TOPK_GROUPS = 4
EXPERT_FF = 256
SHARED_FF = 256
ROUTED_SCALE = 2.5
EXPERT_ROWS = 256
EXPERT_BUFFERS = 4
MOE_TQ = 256
SLOT_ROWS = 8
CHUNK_ROWS = 16
EXPERT_BLOCK = 32
TN_DIMS = (((0,), (0,)), ((), ()))

VMEM_LIMIT = 56 * 1024 * 1024
NEG_INF = float("-inf")


def _cparams(*sem):
    return pltpu.CompilerParams(dimension_semantics=sem, vmem_limit_bytes=VMEM_LIMIT)


def _layer_norm(y, g, b):
    mu = jnp.mean(y, axis=-1, keepdims=True)
    d = y - mu
    var = jnp.mean(d * d, axis=-1, keepdims=True)
    return d * lax.rsqrt(var + LN_EPS) * g + b


def _silu(v):
    return v * jax.nn.sigmoid(v)


def _ab_mixer_kernel(x_ref, win_ref, pmix_ref, pscale_ref, cw_ref, cb_ref, clg_ref, clb_ref,
                     wout_ref, lng_ref, lnb_ref, o_ref, eu_ref, ev_ref, cat_ref, *, tq):
    ti = pl.program_id(1)

    @pl.when(ti == 0)
    def _():
        eu_ref[0:HALO, :] = jnp.zeros((HALO, POOL_WIDTH), F32)
        ev_ref[0:HALO, :] = jnp.zeros((HALO, CONV_CH), F32)

    @pl.when(ti > 0)
    def _():
        eu_ref[0:HALO, :] = eu_ref[tq:tq + HALO, :]
        ev_ref[0:HALO, :] = ev_ref[tq:tq + HALO, :]

    x = x_ref[...]
    h = jnp.dot(x.astype(BF16), win_ref[...], preferred_element_type=F32)
    u = h[:, :POOL_WIDTH]
    eu_ref[HALO:HALO + tq, :] = u
    ev_ref[HALO:HALO + tq, :] = h[:, POOL_WIDTH:POOL_WIDTH + CONV_CH] * jax.nn.sigmoid(
        h[:, POOL_WIDTH + CONV_CH:])

    pos1 = ti * tq + lax.broadcasted_iota(I32, (tq, 1), 0) + 1
    for gi, w in enumerate(POOL_WINDOWS):
        lo, hi = gi * POOL_GROUP, (gi + 1) * POOL_GROUP
        ug = u[:, lo:hi]
        s = ug
        for j in range(1, w):
            s = s + eu_ref[HALO - j:HALO - j + tq, lo:hi]
        cnt = jnp.minimum(pos1, w).astype(F32)
        p = s / cnt - ug
        a = jnp.dot(p.astype(BF16), pmix_ref[gi], preferred_element_type=F32)
        cat_ref[:, lo:hi] = (a * pscale_ref[:, lo:hi]).astype(BF16)

    for r in range(tq // CONV_ROWS):
        base = r * CONV_ROWS + HALO - (CONV_TAPS - 1)
        acc = jnp.broadcast_to(cb_ref[...], (CONV_ROWS, CONV_CH))
        for k in range(CONV_TAPS):
            acc = acc + cw_ref[k:k + 1, :] * ev_ref[base + k:base + k + CONV_ROWS, :]
        c = _silu(_layer_norm(acc, clg_ref[...], clb_ref[...]))
        cat_ref[r * CONV_ROWS:(r + 1) * CONV_ROWS, POOL_WIDTH:] = c.astype(BF16)

    mix = jnp.dot(cat_ref[...], wout_ref[...], preferred_element_type=F32)
    o_ref[...] = _layer_norm(DEEPNORM_ALPHA * x + mix, lng_ref[...], lnb_ref[...])


def ab_mixer(x, w_in, pool_mix, pool_scale, conv_w, conv_b, conv_ln_g, conv_ln_b, w_out, ln_g, ln_b,
             tq=256):
    B, T, D = x.shape
    cw = jnp.pad(conv_w.reshape(CONV_TAPS, CONV_CH), ((0, 32 - CONV_TAPS), (0, 0)))
    row = lambda v: v.reshape(1, -1).astype(F32)
    const = lambda shape: pl.BlockSpec(shape, lambda b, t: (0,) * len(shape))
    return pl.pallas_call(
        functools.partial(_ab_mixer_kernel, tq=tq),
        grid=(B, T // tq),
        in_specs=[
            pl.BlockSpec((None, tq, D), lambda b, t: (b, t, 0)),
            const((D, AB_IN)), const((4, POOL_GROUP, POOL_GROUP)), const((1, POOL_WIDTH)),
            const((32, CONV_CH)), const((1, CONV_CH)), const((1, CONV_CH)), const((1, CONV_CH)),
            const((D, D)), const((1, D)), const((1, D)),
        ],
        out_specs=pl.BlockSpec((None, tq, D), lambda b, t: (b, t, 0)),
        out_shape=jax.ShapeDtypeStruct((B, T, D), F32),
        scratch_shapes=[pltpu.VMEM((HALO + tq, POOL_WIDTH), F32),
                        pltpu.VMEM((HALO + tq, CONV_CH), F32),
                        pltpu.VMEM((tq, D), BF16)],
        compiler_params=_cparams("arbitrary", "arbitrary"),
        name="ab_mixer",
    )(x, w_in.astype(BF16), pool_mix.astype(BF16), row(pool_scale), cw, row(conv_b),
      row(conv_ln_g), row(conv_ln_b), w_out.astype(BF16), row(ln_g), row(ln_b))


def _first_index(hit_src, m, iota, size):
    return jnp.min(jnp.where(hit_src == m, iota, size), axis=0, keepdims=True)


def _router_kernel(x_ref, wh_ref, wl_ref, b_ref, tri_ref, idx_ref, wt_ref, lrank_ref, cnt_ref, *, tq):
    x = x_ref[...]
    xh = x.astype(BF16)
    xl = (x - xh.astype(F32)).astype(BF16)
    nt = (((1,), (1,)), ((), ()))
    wh, wl = wh_ref[...], wl_ref[...]
    logits = (lax.dot_general(wh, xh, nt, preferred_element_type=F32)
              + lax.dot_general(wh, xl, nt, preferred_element_type=F32)
              + lax.dot_general(wl, xh, nt, preferred_element_type=F32))
    scores = jax.nn.sigmoid(logits)
    biased = scores + b_ref[...]

    per_group = N_EXPERTS // N_EXPERT_GROUPS
    io_g = lax.broadcasted_iota(I32, (per_group, tq), 0)
    gs_rows = []
    for g in range(N_EXPERT_GROUPS):
        sg = biased[g * per_group:(g + 1) * per_group, :]
        m1 = jnp.max(sg, axis=0, keepdims=True)
        i1 = _first_index(sg, m1, io_g, per_group)
        m2 = jnp.max(jnp.where(io_g == i1, NEG_INF, sg), axis=0, keepdims=True)
        gs_rows.append(m1 + m2)
    gs = jnp.concatenate(gs_rows, axis=0)

    io_ng = lax.broadcasted_iota(I32, (N_EXPERT_GROUPS, tq), 0)
    grp_sel = jnp.zeros((N_EXPERT_GROUPS, tq), jnp.bool_)
    for _ in range(TOPK_GROUPS):
        m = jnp.max(gs, axis=0, keepdims=True)
        hit = io_ng == _first_index(gs, m, io_ng, N_EXPERT_GROUPS)
        grp_sel = grp_sel | hit
        gs = jnp.where(hit, NEG_INF, gs)
    exp_mask = jnp.concatenate(
        [jnp.broadcast_to(grp_sel[g:g + 1, :], (per_group, tq)) for g in range(N_EXPERT_GROUPS)], axis=0)

    masked = jnp.where(exp_mask, biased, NEG_INF)
    io_e = lax.broadcasted_iota(I32, (N_EXPERTS, tq), 0)
    idx_rows, w_rows = [], []
    onehot = jnp.zeros((N_EXPERTS, tq), F32)
    for _ in range(TOP_K):
        m = jnp.max(masked, axis=0, keepdims=True)
        first = _first_index(masked, m, io_e, N_EXPERTS)
        hit = io_e == first
        idx_rows.append(first)
        w_rows.append(jnp.sum(jnp.where(hit, scores, 0.0), axis=0, keepdims=True))
        onehot = onehot + hit.astype(F32)
        masked = jnp.where(hit, NEG_INF, masked)
    w = jnp.concatenate(w_rows, axis=0)
    wt_ref[...] = w / jnp.sum(w, axis=0, keepdims=True) * ROUTED_SCALE
    idx_ref[...] = jnp.concatenate(idx_rows, axis=0)

    oh = onehot.astype(BF16)
    before = jnp.dot(oh, tri_ref[...], preferred_element_type=F32)
    lrank_ref[...] = jnp.concatenate(
        [jnp.sum(jnp.where(io_e == r, before, 0.0), axis=0, keepdims=True) for r in idx_rows],
        axis=0).astype(I32)
    cnt_ref[...] = lax.dot_general(jnp.ones((8, tq), BF16), oh, nt, preferred_element_type=F32).astype(I32)


def moe_router(x, w_router, b_router, tq):
    N, D = x.shape
    wt = w_router.T
    wh = wt.astype(BF16)
    wl = (wt - wh.astype(F32)).astype(BF16)
    tri = (np.arange(tq)[:, None] < np.arange(tq)[None, :]).astype(np.float32)
    const = lambda shape: pl.BlockSpec(shape, lambda i: (0,) * len(shape))
    tok = lambda rows: pl.BlockSpec((rows, tq), lambda i: (0, i))
    return pl.pallas_call(
        functools.partial(_router_kernel, tq=tq),
        grid=(N // tq,),
        in_specs=[pl.BlockSpec((tq, D), lambda i: (i, 0)), const((N_EXPERTS, D)), const((N_EXPERTS, D)),
                  const((N_EXPERTS, 1)), const((tq, tq))],
        out_specs=[tok(TOP_K), tok(TOP_K), tok(TOP_K), pl.BlockSpec((None, 8, N_EXPERTS), lambda i: (i, 0, 0))],
        out_shape=[jax.ShapeDtypeStruct((TOP_K, N), I32), jax.ShapeDtypeStruct((TOP_K, N), F32),
                   jax.ShapeDtypeStruct((TOP_K, N), I32), jax.ShapeDtypeStruct((N // tq, 8, N_EXPERTS), I32)],
        compiler_params=_cparams("arbitrary"),
        name="moe_router",
    )(x, wh, wl, b_router.reshape(N_EXPERTS, 1).astype(F32), jnp.asarray(tri, BF16))


def _pack_pairs(v):
    half = v.shape[1] // 2
    hi = lax.bitcast_convert_type(v[:, :half], I32)
    lo = lax.bitcast_convert_type(v[:, half:], I32)
    return (hi & jnp.int32(-65536)) | lax.shift_right_logical(lo, jnp.int32(16))


def _unpack_pairs(p):
    hi = lax.bitcast_convert_type(p & jnp.int32(-65536), F32)
    lo = lax.bitcast_convert_type(lax.shift_left(p, jnp.int32(16)), F32)
    return jnp.concatenate([hi, lo], axis=1).astype(BF16)


def _round_bf16(v):
    return v.astype(BF16).astype(F32)


def _slot_copies(cnt_ref, off_ref, hbm, buf, sem, ovf_ref, to_hbm):
    def issue(e, carry):
        n_small, n_big, n_ovf = carry
        c = cnt_ref[0, 0, e]
        off = pl.multiple_of(off_ref[0, 0, e], SLOT_ROWS)
        small = (c > 0) & (c <= SLOT_ROWS)
        big = c > SLOT_ROWS
        over = c > CHUNK_ROWS
        base = pl.multiple_of(e * CHUNK_ROWS, CHUNK_ROWS)

        @pl.when(over)
        def _():
            ovf_ref[n_ovf] = e

        @pl.when(small)
        def _():
            pair = (buf.at[pl.ds(base, SLOT_ROWS), :], hbm.at[pl.ds(off, SLOT_ROWS), :])
            pltpu.make_async_copy(*(pair if to_hbm else pair[::-1]), sem).start()

        @pl.when(big)
        def _():
            pair = (buf.at[pl.ds(base, CHUNK_ROWS), :], hbm.at[pl.ds(off, CHUNK_ROWS), :])
            pltpu.make_async_copy(*(pair if to_hbm else pair[::-1]), sem).start()
        return n_small + small.astype(I32), n_big + big.astype(I32), n_ovf + over.astype(I32)
    return lax.fori_loop(0, N_EXPERTS, issue, (0, 0, 0))


def _slot_waits(n_small, n_big, hbm, buf, sem, to_hbm):
    def drain(rows):
        def body(_, c):
            pair = (buf.at[pl.ds(0, rows), :], hbm.at[pl.ds(0, rows), :])
            pltpu.make_async_copy(*(pair if to_hbm else pair[::-1]), sem).wait()
            return c
        return body
    lax.fori_loop(0, n_small, drain(SLOT_ROWS), 0)
    lax.fori_loop(0, n_big, drain(CHUNK_ROWS), 0)


def _slot_rows(lrk_at, fill, tq):
    io_r = lax.broadcasted_iota(I32, (CHUNK_ROWS, tq), 0)
    blocks = []
    for eb in range(N_EXPERTS // EXPERT_BLOCK):
        rows = []
        for e in range(eb * EXPERT_BLOCK, (eb + 1) * EXPERT_BLOCK):
            hit = jnp.broadcast_to(lrk_at[e:e + 1, :], (CHUNK_ROWS, tq)) == io_r
            rows.append(fill(e, hit))
        blocks.append(jnp.concatenate(rows, axis=0))
    return blocks


def _dispatch_kernel(cnt_ref, off_ref, x_ref, idx_ref, lrk_ref, xs_hbm, xbuf, obuf, ovf_ref, sem, *, tq):
    xb = x_ref[...].astype(BF16)
    io_e = lax.broadcasted_iota(I32, (N_EXPERTS, tq), 0)
    lrk_at = jnp.full((N_EXPERTS, tq), -1, I32)
    for k in range(TOP_K):
        lrk_at = jnp.where(io_e == idx_ref[k:k + 1, :], lrk_ref[k:k + 1, :], lrk_at)
    rows_per_block = EXPERT_BLOCK * CHUNK_ROWS
    for eb, pick in enumerate(_slot_rows(lrk_at, lambda e, hit: hit.astype(BF16), tq)):
        xbuf[eb * rows_per_block:(eb + 1) * rows_per_block, :] = _pack_pairs(
            jnp.dot(pick, xb, preferred_element_type=F32))
    n_small, n_big, n_ovf = _slot_copies(cnt_ref, off_ref, xs_hbm, xbuf, sem, ovf_ref, True)
    _slot_waits(n_small, n_big, xs_hbm, xbuf, sem, True)

    io_r = lax.broadcasted_iota(I32, (SLOT_ROWS, tq), 0)

    def overflow(i, c):
        e = ovf_ref[i]
        n_units = (cnt_ref[0, 0, e] + SLOT_ROWS - 1) // SLOT_ROWS
        off = pl.multiple_of(off_ref[0, 0, e], SLOT_ROWS)

        def extra(j, c2):
            rank_e = jnp.full((1, tq), -1, I32)
            for k in range(TOP_K):
                rank_e = jnp.where(idx_ref[k:k + 1, :] == e, lrk_ref[k:k + 1, :], rank_e)
            pick = (io_r + j * SLOT_ROWS == rank_e).astype(BF16)
            obuf[...] = _pack_pairs(jnp.dot(pick, xb, preferred_element_type=F32))
            cp = pltpu.make_async_copy(obuf, xs_hbm.at[pl.ds(off + j * SLOT_ROWS, SLOT_ROWS), :], sem)
            cp.start()
            cp.wait()
            return c2
        return lax.fori_loop(CHUNK_ROWS // SLOT_ROWS, n_units, extra, c)
    lax.fori_loop(0, n_ovf, overflow, 0)


def moe_dispatch(x, idx, lrank, tile_cnt, tile_off, n_rows, tq):
    N, D = x.shape
    n_tiles = N // tq
    smem = lambda: pl.BlockSpec((1, 1, N_EXPERTS), lambda i: (i, 0, 0), memory_space=pltpu.SMEM)
    return pl.pallas_call(
        functools.partial(_dispatch_kernel, tq=tq),
        grid=(n_tiles,),
        in_specs=[smem(), smem(), pl.BlockSpec((tq, D), lambda i: (i, 0)),
                  pl.BlockSpec((TOP_K, tq), lambda i: (0, i)), pl.BlockSpec((TOP_K, tq), lambda i: (0, i))],
        out_specs=pl.BlockSpec(memory_space=pl.ANY),
        out_shape=jax.ShapeDtypeStruct((n_rows, D // 2), I32),
        scratch_shapes=[pltpu.VMEM((N_EXPERTS * CHUNK_ROWS, D // 2), I32), pltpu.VMEM((SLOT_ROWS, D // 2), I32),
                        pltpu.SMEM((N_EXPERTS,), I32), pltpu.SemaphoreType.DMA(())],
        compiler_params=_cparams("arbitrary"),
        name="moe_dispatch",
    )(tile_cnt.reshape(n_tiles, 1, N_EXPERTS), tile_off.reshape(n_tiles, 1, N_EXPERTS), x, idx, lrank)


def _expert_kernel(row0_ref, nblk_ref, used_ref, x_hbm, wg_ref, wu_ref, wd_ref, y_hbm, wgb, wub, wdb, xin, yout,
                   in_sem, out_sem):
    e = pl.program_id(0)
    m = EXPERT_ROWS
    nb = nblk_ref[e]
    row0 = row0_ref[e]
    rows_of = lambda j: pl.ds(pl.multiple_of(row0 + j * m, m), m)
    in_copy = lambda j, slot: pltpu.make_async_copy(x_hbm.at[rows_of(j), :], xin.at[slot], in_sem.at[slot])
    out_copy = lambda j, slot: pltpu.make_async_copy(yout.at[slot], y_hbm.at[rows_of(j), :], out_sem.at[slot])

    @pl.when(nb > 0)
    def _():
        for j0 in range(EXPERT_BUFFERS - 1):
            @pl.when(j0 < nb)
            def _(j0=j0):
                in_copy(j0, j0).start()
        wgb[...] = wg_ref[...].astype(BF16)
        wub[...] = wu_ref[...].astype(BF16)
        wdb[...] = wd_ref[...].astype(BF16)

        def block(j, c):
            slot = j % EXPERT_BUFFERS
            in_copy(j, slot).wait()
            ahead = j + EXPERT_BUFFERS - 1

            @pl.when(ahead < nb)
            def _():
                in_copy(ahead, ahead % EXPERT_BUFFERS).start()

            @pl.when(j >= EXPERT_BUFFERS)
            def _():
                out_copy(j - EXPERT_BUFFERS, slot).wait()

            rows = lax.broadcasted_iota(I32, (m, xin.shape[2]), 0)
            xb = _unpack_pairs(jnp.where(rows < used_ref[e] - j * m, xin[slot], 0))
            g = jnp.dot(xb, wgb[...], preferred_element_type=F32)
            u = jnp.dot(xb, wub[...], preferred_element_type=F32)
            hid = (_silu(g) * u).astype(BF16)
            yout[slot] = _pack_pairs(_round_bf16(jnp.dot(hid, wdb[...], preferred_element_type=F32)))
            out_copy(j, slot).start()
            return c
        lax.fori_loop(0, nb, block, 0)

        for back in range(1, EXPERT_BUFFERS + 1):
            @pl.when(nb >= back)
            def _(back=back):
                out_copy(nb - back, (nb - back) % EXPERT_BUFFERS).wait()


def moe_experts(xs, row0, nblk, used, w_gate, w_up, w_down, layer):
    m = EXPERT_ROWS
    D = xs.shape[1] * 2
    wspec = lambda r, c: pl.BlockSpec((None, None, r, c), lambda e, *_: (layer, e, 0, 0))
    grid_spec = pltpu.PrefetchScalarGridSpec(
        num_scalar_prefetch=3,
        grid=(N_EXPERTS,),
        in_specs=[pl.BlockSpec(memory_space=pl.ANY), wspec(D, EXPERT_FF), wspec(D, EXPERT_FF), wspec(EXPERT_FF, D)],
        out_specs=pl.BlockSpec(memory_space=pl.ANY),
        scratch_shapes=[pltpu.VMEM((D, EXPERT_FF), BF16), pltpu.VMEM((D, EXPERT_FF), BF16),
                        pltpu.VMEM((EXPERT_FF, D), BF16), pltpu.VMEM((EXPERT_BUFFERS, m, D // 2), I32),
                        pltpu.VMEM((EXPERT_BUFFERS, m, D // 2), I32), pltpu.SemaphoreType.DMA((EXPERT_BUFFERS,)),
                        pltpu.SemaphoreType.DMA((EXPERT_BUFFERS,))],
    )
    return pl.pallas_call(
        _expert_kernel,
        grid_spec=grid_spec,
        out_shape=jax.ShapeDtypeStruct(xs.shape, I32),
        compiler_params=_cparams("arbitrary"),
        name="moe_experts",
    )(row0, nblk, used, xs, w_gate, w_up, w_down)


def _combine_kernel(cnt_ref, off_ref, y_hbm, idx_ref, lrk_ref, w_ref, x_ref, wsg_ref, wsu_ref, wsd_ref,
                    lng_ref, lnb_ref, o_ref, ybuf, obuf, acc_ref, ovf_ref, sem, *, tq):
    @pl.when(pl.program_id(0) == 0)
    def _():
        ybuf[...] = jnp.zeros_like(ybuf)

    n_small, n_big, n_ovf = _slot_copies(cnt_ref, off_ref, y_hbm, ybuf, sem, ovf_ref, False)

    x = x_ref[...]
    xb = x.astype(BF16)
    g = jnp.dot(xb, wsg_ref[...], preferred_element_type=F32)
    u = jnp.dot(xb, wsu_ref[...], preferred_element_type=F32)
    acc_ref[...] = jnp.dot((_silu(g) * u).astype(BF16), wsd_ref[...], preferred_element_type=F32)

    io_e = lax.broadcasted_iota(I32, (N_EXPERTS, tq), 0)
    w_at = jnp.zeros((N_EXPERTS, tq), F32)
    lrk_at = jnp.full((N_EXPERTS, tq), -1, I32)
    for k in range(TOP_K):
        hit = io_e == idx_ref[k:k + 1, :]
        w_at = jnp.where(hit, w_ref[k:k + 1, :], w_at)
        lrk_at = jnp.where(hit, lrk_ref[k:k + 1, :], lrk_at)
    fill = lambda e, hit: jnp.where(hit, jnp.broadcast_to(w_at[e:e + 1, :], hit.shape), 0.0).astype(BF16)
    weights = _slot_rows(lrk_at, fill, tq)

    _slot_waits(n_small, n_big, y_hbm, ybuf, sem, False)

    rows_per_block = EXPERT_BLOCK * CHUNK_ROWS
    for eb, wct in enumerate(weights):
        yb = _unpack_pairs(ybuf[eb * rows_per_block:(eb + 1) * rows_per_block, :])
        acc_ref[...] += lax.dot_general(wct, yb, TN_DIMS, preferred_element_type=F32)

    io_r = lax.broadcasted_iota(I32, (SLOT_ROWS, tq), 0)

    def overflow(i, c):
        e = ovf_ref[i]
        n_units = (cnt_ref[0, 0, e] + SLOT_ROWS - 1) // SLOT_ROWS
        off = pl.multiple_of(off_ref[0, 0, e], SLOT_ROWS)

        def extra(j, c2):
            cp = pltpu.make_async_copy(y_hbm.at[pl.ds(off + j * SLOT_ROWS, SLOT_ROWS), :], obuf, sem)
            cp.start()
            w_e = jnp.zeros((1, tq), F32)
            rank_e = jnp.full((1, tq), -1, I32)
            for k in range(TOP_K):
                hit = idx_ref[k:k + 1, :] == e
                w_e = jnp.where(hit, w_ref[k:k + 1, :], w_e)
                rank_e = jnp.where(hit, lrk_ref[k:k + 1, :], rank_e)
            wct = jnp.where(io_r + j * SLOT_ROWS == rank_e, w_e, 0.0).astype(BF16)
            cp.wait()
            acc_ref[...] += lax.dot_general(wct, _unpack_pairs(obuf[...]), TN_DIMS, preferred_element_type=F32)
            return c2
        return lax.fori_loop(CHUNK_ROWS // SLOT_ROWS, n_units, extra, c)
    lax.fori_loop(0, n_ovf, overflow, 0)

    o_ref[...] = _layer_norm(DEEPNORM_ALPHA * x + acc_ref[...], lng_ref[...], lnb_ref[...])


def moe_combine(x, ys, idx, lrank, wts, tile_cnt, tile_off, ws_gate, ws_up, ws_down, ln_g, ln_b, tq):
    N, D = x.shape
    n_tiles = N // tq
    row = lambda v: v.reshape(1, -1).astype(F32)
    const = lambda shape: pl.BlockSpec(shape, lambda i: (0,) * len(shape))
    smem = lambda: pl.BlockSpec((1, 1, N_EXPERTS), lambda i: (i, 0, 0), memory_space=pltpu.SMEM)
    tokl = lambda: pl.BlockSpec((TOP_K, tq), lambda i: (0, i))
    return pl.pallas_call(
        functools.partial(_combine_kernel, tq=tq),
        grid=(n_tiles,),
        in_specs=[smem(), smem(), pl.BlockSpec(memory_space=pl.ANY), tokl(), tokl(), tokl(),
                  pl.BlockSpec((tq, D), lambda i: (i, 0)),
                  const((D, SHARED_FF)), const((D, SHARED_FF)), const((SHARED_FF, D)), const((1, D)), const((1, D))],
        out_specs=pl.BlockSpec((tq, D), lambda i: (i, 0)),
        out_shape=jax.ShapeDtypeStruct((N, D), F32),
        scratch_shapes=[pltpu.VMEM((N_EXPERTS * CHUNK_ROWS, D // 2), I32), pltpu.VMEM((SLOT_ROWS, D // 2), I32),
                        pltpu.VMEM((tq, D), F32), pltpu.SMEM((N_EXPERTS,), I32), pltpu.SemaphoreType.DMA(())],
        compiler_params=_cparams("arbitrary"),
        name="moe_combine",
    )(tile_cnt.reshape(n_tiles, 1, N_EXPERTS), tile_off.reshape(n_tiles, 1, N_EXPERTS), ys, idx, lrank,
      wts, x, ws_gate.astype(BF16), ws_up.astype(BF16), ws_down.astype(BF16), row(ln_g), row(ln_b))


def moe_layer(x, layer, w_router, b_router, w_gate, w_up, w_down, ws_gate, ws_up, ws_down, ln_g, ln_b):
    N, D = x.shape
    m = EXPERT_ROWS
    tq = MOE_TQ
    idx, wts, lrank, tile_cnt = moe_router(x, w_router, b_router, tq)
    tile_cnt = tile_cnt[:, 0, :]
    slots = (tile_cnt + SLOT_ROWS - 1) // SLOT_ROWS * SLOT_ROWS
    used = jnp.sum(slots, axis=0)
    padded = (used + m - 1) // m * m
    pend = jnp.cumsum(padded)
    pstart = pend - padded
    tile_off = pstart[None, :] + jnp.cumsum(slots, axis=0) - slots
    n_assign = N * TOP_K
    max_rows = n_assign + (SLOT_ROWS - 1) * min(n_assign, (N // tq) * N_EXPERTS) + N_EXPERTS * (m - 1)
    xs = moe_dispatch(x, idx, lrank, tile_cnt, tile_off, -(-max_rows // m) * m, tq)
    ys = moe_experts(xs, pstart.astype(I32), (padded // m).astype(I32), used.astype(I32), w_gate, w_up, w_down, layer)
    return moe_combine(x, ys, idx, lrank, wts, tile_cnt, tile_off, ws_gate, ws_up, ws_down, ln_g, ln_b, tq)


Q_TILE = 128
SEL_KEYS = 512
WIN_TILE = 128
WIN_KEYS = WINDOW + Q_TILE
PROJ_TQ = SEL_KEYS
KS_WIDTH = N_KV_GROUPS * 128
VS_ROWS = HEAD_DIM + 16
NSA_TOK_COLS = N_HEADS * HEAD_DIM + 3 * KV_WIDTH + KS_WIDTH
NSA_FEAT_ROWS = 2 * KV_WIDTH + N_HEADS * N_BRANCH
NT_DIMS = (((1,), (1,)), ((), ()))
LOG2E = 1.4426950408889634
MASK_BIAS = -1e30


def _rotary(v, cos, sin, lo_mask):
    parts = []
    for c in range(v.shape[1] // 128):
        blk = v[:, c * 128:(c + 1) * 128]
        nxt = pltpu.roll(blk, 128 - ROPE_DIM // 2, axis=1)
        prv = pltpu.roll(blk, ROPE_DIM // 2, axis=1)
        parts.append(blk * cos + jnp.where(lo_mask, nxt, prv) * sin)
    return jnp.concatenate(parts, axis=1)


def _nsa_proj_kernel(x_ref, wtok_ref, wfeat_ref, cos_ref, sin_ref, cosk_ref, sink_ref, q_ref, qr_ref, kc_ref,
                     vc_ref, ks_ref, kw_ref, vs_ref, vw_ref, g_ref, *, tq):
    xb = x_ref[...].astype(BF16)
    h = jnp.dot(xb, wtok_ref[...], preferred_element_type=F32)
    cos, sin = cos_ref[...], sin_ref[...]
    lane = lax.broadcasted_iota(I32, (tq, 128), 1)
    lo_mask = (lane % HEAD_DIM) < ROPE_DIM // 2
    nq = N_HEADS * HEAD_DIM
    q = h[:, :nq] * (HEAD_DIM ** -0.5 * LOG2E)
    q_ref[...] = q.astype(BF16)
    qr_ref[...] = _rotary(q, cos, sin, lo_mask).astype(BF16)
    c0 = nq
    kc_ref[...] = h[:, c0:c0 + KV_WIDTH].astype(BF16)
    vc_ref[...] = h[:, c0 + KV_WIDTH:c0 + 2 * KV_WIDTH].astype(BF16)
    c0 += 2 * KV_WIDTH
    ks = _rotary(h[:, c0:c0 + N_KV_GROUPS * 128], cosk_ref[...], sink_ref[...], lo_mask)
    blk = lax.broadcasted_iota(I32, (tq, 128), 0) % SEL_KEYS // SLC_BLOCK
    onehot = (lane - HEAD_DIM == blk).astype(F32)
    ks_ref[...] = (ks + jnp.concatenate([onehot] * N_KV_GROUPS, axis=1)).astype(BF16)
    c0 += N_KV_GROUPS * 128
    kw_ref[...] = _rotary(h[:, c0:c0 + KV_WIDTH], cos, sin, lo_mask).astype(BF16)
    hf = lax.dot_general(wfeat_ref[...], xb, NT_DIMS, preferred_element_type=F32)
    ones_row = (lax.broadcasted_iota(I32, (VS_ROWS - HEAD_DIM, tq), 0) == 0).astype(BF16)
    for g in range(N_KV_GROUPS):
        vs_ref[g, 0:HEAD_DIM, :] = hf[g * HEAD_DIM:(g + 1) * HEAD_DIM].astype(BF16)
        vs_ref[g, HEAD_DIM:VS_ROWS, :] = ones_row
    for j in range(tq // WIN_TILE):
        vw_ref[j] = hf[KV_WIDTH:2 * KV_WIDTH, j * WIN_TILE:(j + 1) * WIN_TILE].astype(BF16)
    g_ref[...] = jax.nn.sigmoid(hf[2 * KV_WIDTH:])


def nsa_project(x, w_in, tq=PROJ_TQ):
    B, T, D = x.shape
    nq = N_HEADS * HEAD_DIM
    q, kc, vc, ks, vs, kw, vw, g = jnp.split(w_in, [nq + i * KV_WIDTH for i in range(7)], axis=1)
    ks_wide = jnp.pad(ks.reshape(D, N_KV_GROUPS, HEAD_DIM), ((0, 0), (0, 0), (0, 128 - HEAD_DIM)))
    wtok = jnp.concatenate([q, kc, vc, ks_wide.reshape(D, KS_WIDTH), kw], axis=1).astype(BF16)
    wfeat = jnp.concatenate([vs, vw, g], axis=1).T.astype(BF16)
    inv_freq = jnp.power(jnp.float32(ROPE_THETA), -jnp.arange(0, ROPE_DIM, 2, dtype=F32) / ROPE_DIM)
    ang = jnp.arange(T).astype(F32)[:, None] * inv_freq[None, :]
    ones = jnp.ones((T, HEAD_DIM - ROPE_DIM), F32)
    cos_h = jnp.concatenate([jnp.cos(ang), jnp.cos(ang), ones], axis=1)
    sin_h = jnp.concatenate([-jnp.sin(ang), jnp.sin(ang), 0.0 * ones], axis=1)
    ident = jnp.ones((T, 128 - HEAD_DIM), F32)
    tables = [jnp.concatenate([cos_h, cos_h], axis=1), jnp.concatenate([sin_h, sin_h], axis=1),
              jnp.concatenate([cos_h, ident], axis=1), jnp.concatenate([sin_h, 0.0 * ident], axis=1)]
    nt = T // tq
    tokm = lambda w: pl.BlockSpec((None, tq, w), lambda b, t: (b, t, 0))
    const = lambda shape: pl.BlockSpec(shape, lambda b, t: (0,) * len(shape))
    table = lambda: pl.BlockSpec((tq, 128), lambda b, t: (t, 0))
    bf = lambda *shape: jax.ShapeDtypeStruct(shape, BF16)
    return pl.pallas_call(
        functools.partial(_nsa_proj_kernel, tq=tq),
        grid=(B, nt),
        in_specs=[tokm(D), const((D, NSA_TOK_COLS)), const((NSA_FEAT_ROWS, D)), table(), table(), table(), table()],
        out_specs=[tokm(nq), tokm(nq), tokm(KV_WIDTH), tokm(KV_WIDTH), tokm(KS_WIDTH), tokm(KV_WIDTH),
                   pl.BlockSpec((None, None, N_KV_GROUPS, VS_ROWS, tq), lambda b, t: (b, t, 0, 0, 0)),
                   pl.BlockSpec((None, tq // WIN_TILE, KV_WIDTH, WIN_TILE), lambda b, t: (b, t, 0, 0)),
                   pl.BlockSpec((None, N_HEADS * N_BRANCH, tq), lambda b, t: (b, 0, t))],
        out_shape=[bf(B, T, nq), bf(B, T, nq), bf(B, T, KV_WIDTH), bf(B, T, KV_WIDTH), bf(B, T, KS_WIDTH),
                   bf(B, T, KV_WIDTH), bf(B, T // tq, N_KV_GROUPS, VS_ROWS, tq),
                   bf(B, T // WIN_TILE, KV_WIDTH, WIN_TILE),
                   jax.ShapeDtypeStruct((B, N_HEADS * N_BRANCH, T), F32)],
        compiler_params=_cparams("arbitrary", "arbitrary"),
        name="nsa_project",
    )(x, wtok, wfeat, *tables)


def _compress_kernel(x_ref, pos_ref, w1_ref, w2_ref, o_ref):
    blocks = (x_ref[...].astype(F32) + pos_ref[...]).astype(BF16)
    pre = jnp.dot(blocks, w1_ref[...], preferred_element_type=F32)
    hid = jax.nn.gelu(pre, approximate=True)
    o_ref[...] = jnp.dot(hid.astype(BF16), w2_ref[...], preferred_element_type=F32).astype(BF16)


def nsa_compress(kv, pos_emb, w1, w2, tr=512):
    B, T, _ = kv.shape
    nch = T // CMP_STRIDE
    ch = kv.reshape(B, nch, CMP_STRIDE, N_KV_GROUPS, HEAD_DIM)
    blocks = jnp.concatenate([ch[:, :-1], ch[:, 1:]], axis=2)
    flat = jnp.moveaxis(blocks, 3, 2).reshape(B, nch - 1, N_KV_GROUPS, CMP_BLOCK * HEAD_DIM)
    flat = jnp.pad(flat, ((0, 0), (0, 1), (0, 0), (0, 0))).reshape(B * nch * N_KV_GROUPS, CMP_BLOCK * HEAD_DIM)
    rows = flat.shape[0]
    tr = min(tr, rows)
    const = lambda shape: pl.BlockSpec(shape, lambda i: (0,) * len(shape))
    out = pl.pallas_call(
        _compress_kernel,
        grid=(rows // tr,),
        in_specs=[pl.BlockSpec((tr, CMP_BLOCK * HEAD_DIM), lambda i: (i, 0)), const((1, CMP_BLOCK * HEAD_DIM)),
                  const((CMP_BLOCK * HEAD_DIM, CMP_HIDDEN)), const((CMP_HIDDEN, HEAD_DIM))],
        out_specs=pl.BlockSpec((tr, HEAD_DIM), lambda i: (i, 0)),
        out_shape=jax.ShapeDtypeStruct((rows, HEAD_DIM), BF16),
        compiler_params=_cparams("arbitrary"),
        name="nsa_compress",
    )(flat, pos_emb.reshape(1, -1).astype(F32), w1.astype(BF16), w2.astype(BF16))
    return out.reshape(B, nch, N_KV_GROUPS * HEAD_DIM)


def _softmax_keys(s, mask):
    s = jnp.where(mask, s, NEG_INF)
    m = jnp.max(s, axis=0, keepdims=True)
    m = jnp.where((m > NEG_INF) & (m < float("inf")), m, 0.0)
    e = jnp.exp2(s - m)
    d = jnp.sum(e, axis=0, keepdims=True)
    return e, 1.0 / jnp.where(d > 0, d, 1.0)


def _nsa_attn_kernel(q_ref, qr_ref, kc_ref, vct_ref, ks_ref, vst_ref, kw_ref, vwt_ref, g_ref, mt_ref, o_ref,
                     bias_ref, *, qb):
    t0 = pl.program_id(1) * qb
    hpg = HEADS_PER_GROUP
    t_row = t0 + lax.broadcasted_iota(I32, (1, qb), 1)
    t_all = jnp.concatenate([t_row] * hpg, axis=1)
    q = q_ref[...]
    qr_t = qr_ref[...].astype(F32).T
    gates = g_ref[...]
    n_cmp = kc_ref.shape[0]
    n_slc = mt_ref.shape[0]
    blocks_per_tile = SEL_KEYS // SLC_BLOCK
    o_cmp_all = []
    for g in range(N_KV_GROUPS):
        lanes = slice(g * HEAD_DIM, (g + 1) * HEAD_DIM)
        qg = jnp.concatenate(
            [q[:, (g * hpg + hp) * HEAD_DIM:(g * hpg + hp + 1) * HEAD_DIM] for hp in range(hpg)], axis=0)

        sc = lax.dot_general(kc_ref[:, lanes], qg, NT_DIMS, preferred_element_type=F32)
        cmp_end = lax.broadcasted_iota(I32, (n_cmp, 1), 0) * CMP_STRIDE + (CMP_BLOCK - 1)
        ec, inv_c = _softmax_keys(sc, cmp_end <= t_all)
        pc = ec * inv_c
        o_cmp_all.append(jnp.dot(vct_ref[lanes, :], pc.astype(BF16), preferred_element_type=F32))

        pcs = pc[:, :qb]
        for hp in range(1, hpg):
            pcs = pcs + pc[:, hp * qb:(hp + 1) * qb]
        hi = pcs.astype(BF16)
        r1 = pcs - hi.astype(F32)
        mid = r1.astype(BF16)
        low = (r1 - mid.astype(F32)).astype(BF16)
        mt = mt_ref[...]
        imp = (jnp.dot(mt, hi, preferred_element_type=F32) + jnp.dot(mt, mid, preferred_element_type=F32)
               + jnp.dot(mt, low, preferred_element_type=F32))
        j_io = lax.broadcasted_iota(I32, (n_slc, 1), 0)
        cur = t_row // SLC_BLOCK
        forced = (j_io == 0) | (j_io == cur) | (j_io == cur - 1)
        imp = jnp.where(forced, float("inf"), jnp.where(j_io * SLC_BLOCK <= t_row, imp, NEG_INF))
        sel = jnp.zeros((n_slc, qb), jnp.bool_)
        for _ in range(min(N_SELECT, n_slc)):
            m = jnp.max(imp, axis=0, keepdims=True)
            hit = j_io == jnp.min(jnp.where(imp == m, j_io, n_slc), axis=0, keepdims=True)
            sel = sel | hit
            imp = jnp.where(hit, NEG_INF, imp)
        bias_ref[g] = jnp.where(sel, 0.0, MASK_BIAS)

    q_ts = [jnp.concatenate([qr_t[(g * hpg + hp) * HEAD_DIM:(g * hpg + hp + 1) * HEAD_DIM, :]
                             for hp in range(hpg)], axis=1) for g in range(N_KV_GROUPS)]
    zero_rows = jnp.zeros((128 - HEAD_DIM - blocks_per_tile, hpg * qb), F32)

    def sel_tile(kt, carry, causal):
        k0 = pl.multiple_of(kt * SEL_KEYS, SEL_KEYS)
        b0 = pl.multiple_of(kt * blocks_per_tile, blocks_per_tile)
        out = []
        scores = []
        for g in range(N_KV_GROUPS):
            mask_rows = bias_ref[g, pl.ds(b0, blocks_per_tile), :]
            rhs = jnp.concatenate([q_ts[g], jnp.concatenate([mask_rows] * hpg, axis=1), zero_rows], axis=0)
            scores.append(jnp.dot(ks_ref[pl.ds(k0, SEL_KEYS), g * 128:(g + 1) * 128], rhs.astype(BF16),
                                  preferred_element_type=F32))
        for g in range(N_KV_GROUPS):
            m_run, acc = carry[g]
            s = scores[g]
            if causal:
                kpos = k0 + lax.broadcasted_iota(I32, (SEL_KEYS, 1), 0)
                s = jnp.where(kpos <= t_all, s, MASK_BIAS)
            m_new = jnp.maximum(m_run, jnp.max(s, axis=0, keepdims=True))
            p = jnp.exp2(s - m_new)
            pv = jnp.dot(vst_ref[kt, g], p.astype(BF16), preferred_element_type=F32)
            out.append((m_new, jnp.exp2(m_run - m_new) * acc + pv))
        return tuple(out)

    init = tuple((jnp.full((1, hpg * qb), NEG_INF, F32), jnp.zeros((VS_ROWS, hpg * qb), F32))
                 for _ in range(N_KV_GROUPS))
    last = t0 // SEL_KEYS
    carry = lax.fori_loop(0, last, functools.partial(sel_tile, causal=False), init)
    sel_out = sel_tile(last, carry, True)

    heads = []
    for g in range(N_KV_GROUPS):
        lanes = slice(g * HEAD_DIM, (g + 1) * HEAD_DIM)
        q_t = q_ts[g]
        acc = sel_out[g][1]
        l_fin = acc[HEAD_DIM:HEAD_DIM + 1, :]
        o_sel = acc[:HEAD_DIM, :] * (1.0 / jnp.where(l_fin > 0, l_fin, 1.0))

        start = pl.multiple_of(jnp.maximum(t0 - WINDOW, 0), WIN_TILE)
        sw = jnp.dot(kw_ref[pl.ds(start, WIN_KEYS), lanes], q_t.astype(BF16), preferred_element_type=F32)
        dlt = t_all - (start + lax.broadcasted_iota(I32, (WIN_KEYS, 1), 0))
        ew, inv_w = _softmax_keys(sw, (dlt >= 0) & (dlt < WINDOW))
        w0 = start // WIN_TILE
        vwb = jnp.concatenate([vwt_ref[w0 + j, lanes, :] for j in range(WIN_KEYS // WIN_TILE)], axis=1)
        o_win = jnp.dot(vwb, ew.astype(BF16), preferred_element_type=F32) * inv_w

        o_cmp = o_cmp_all[g]
        for hp in range(hpg):
            r = (g * hpg + hp) * N_BRANCH
            cols = slice(hp * qb, (hp + 1) * qb)
            heads.append(gates[r:r + 1] * o_cmp[:, cols] + gates[r + 1:r + 2] * o_sel[:, cols]
                         + gates[r + 2:r + 3] * o_win[:, cols])
    o_ref[...] = jnp.concatenate(heads, axis=0).T.astype(BF16)


def nsa_attention(q, qr, kc, vct, ks, vst, kw, vwt, gates, qb=Q_TILE):
    B, T, nq = q.shape
    n_cmp = kc.shape[1]
    n_slc = T // SLC_BLOCK
    mt = np.zeros((n_slc, n_cmp), np.float32)
    per = SLC_BLOCK // CMP_STRIDE
    for j in range(n_slc):
        for k, wk in enumerate(SLC_OVERLAP_W):
            n = per * j + k - 1
            if 0 <= n < n_cmp - 1:
                mt[j, n] = wk
    per_b = lambda shape: pl.BlockSpec((None,) + shape, lambda b, i: (b,) + (0,) * len(shape))
    return pl.pallas_call(
        functools.partial(_nsa_attn_kernel, qb=qb),
        grid=(B, T // qb),
        in_specs=[pl.BlockSpec((None, qb, nq), lambda b, i: (b, i, 0)),
                  pl.BlockSpec((None, qb, nq), lambda b, i: (b, i, 0)),
                  per_b((n_cmp, KV_WIDTH)), per_b((KV_WIDTH, n_cmp)),
                  per_b((T, KS_WIDTH)), per_b((T // SEL_KEYS, N_KV_GROUPS, VS_ROWS, SEL_KEYS)),
                  per_b((T, KV_WIDTH)), per_b((T // WIN_TILE, KV_WIDTH, WIN_TILE)),
                  pl.BlockSpec((None, N_HEADS * N_BRANCH, qb), lambda b, i: (b, 0, i)),
                  pl.BlockSpec((n_slc, n_cmp), lambda b, i: (0, 0))],
        out_specs=pl.BlockSpec((None, qb, nq), lambda b, i: (b, i, 0)),
        out_shape=jax.ShapeDtypeStruct((B, T, nq), BF16),
        scratch_shapes=[pltpu.VMEM((N_KV_GROUPS, n_slc, qb), F32)],
        compiler_params=_cparams("arbitrary", "arbitrary"),
        name="nsa_attention",
    )(q, qr, kc, vct, ks, vst, kw, vwt, gates, jnp.asarray(mt, BF16))


def _proj_ln_kernel(a_ref, w_ref, x_ref, lng_ref, lnb_ref, o_ref):
    mix = jnp.dot(a_ref[...], w_ref[...], preferred_element_type=F32)
    o_ref[...] = _layer_norm(DEEPNORM_ALPHA * x_ref[...] + mix, lng_ref[...], lnb_ref[...])


def proj_ln(a, w, x, ln_g, ln_b, tq=512):
    N, D = x.shape
    K = a.shape[1]
    row = lambda v: v.reshape(1, -1).astype(F32)
    const = lambda shape: pl.BlockSpec(shape, lambda i: (0,) * len(shape))
    return pl.pallas_call(
        _proj_ln_kernel,
        grid=(N // tq,),
        in_specs=[pl.BlockSpec((tq, K), lambda i: (i, 0)), const((K, D)), pl.BlockSpec((tq, D), lambda i: (i, 0)),
                  const((1, D)), const((1, D))],
        out_specs=pl.BlockSpec((tq, D), lambda i: (i, 0)),
        out_shape=jax.ShapeDtypeStruct((N, D), F32),
        compiler_params=_cparams("arbitrary"),
        name="proj_ln",
    )(a, w.astype(BF16), x, row(ln_g), row(ln_b))


def nsa_layer(x, w_in, cmp_pos_k, cmp_w1_k, cmp_w2_k, cmp_pos_v, cmp_w1_v, cmp_w2_v, w_out, ln_g, ln_b):
    B, T, D = x.shape
    q, qr, kc, vc, ks, kw, vst, vwt, gates = nsa_project(x, w_in)
    kc_cmp = nsa_compress(kc, cmp_pos_k, cmp_w1_k, cmp_w2_k)
    vc_cmp = nsa_compress(vc, cmp_pos_v, cmp_w1_v, cmp_w2_v)
    o = nsa_attention(q, qr, kc_cmp, jnp.swapaxes(vc_cmp, 1, 2), ks, vst, kw, vwt, gates)
    return proj_ln(o.reshape(B * T, -1), w_out, x.reshape(B * T, D), ln_g, ln_b)


def kernel(x, ab_w_in, ab_pool_mix, ab_pool_scale, ab_conv_w, ab_conv_b, ab_conv_ln_g, ab_conv_ln_b, ab_w_out, nsa_w_in, nsa_cmp_pos_k, nsa_cmp_w1_k, nsa_cmp_w2_k, nsa_cmp_pos_v, nsa_cmp_w1_v, nsa_cmp_w2_v, nsa_w_out, ln_mix_g, ln_mix_b, ln_ffn_g, ln_ffn_b, moe_w_router, moe_b_router, moe_w_gate, moe_w_up, moe_w_down, moe_ws_gate, moe_ws_up, moe_ws_down):
    B, T, D = x.shape
    moe = lambda h, i: moe_layer(h, i, moe_w_router[i], moe_b_router[i], moe_w_gate, moe_w_up, moe_w_down,
                                 moe_ws_gate[i], moe_ws_up[i], moe_ws_down[i], ln_ffn_g[i], ln_ffn_b[i])
    h = ab_mixer(x, ab_w_in[0], ab_pool_mix[0], ab_pool_scale[0], ab_conv_w[0], ab_conv_b[0], ab_conv_ln_g[0],
                 ab_conv_ln_b[0], ab_w_out[0], ln_mix_g[0], ln_mix_b[0])
    h = moe(h.reshape(B * T, D), 0)
    h = nsa_layer(h.reshape(B, T, D), nsa_w_in[0], nsa_cmp_pos_k[0], nsa_cmp_w1_k[0], nsa_cmp_w2_k[0],
                  nsa_cmp_pos_v[0], nsa_cmp_w1_v[0], nsa_cmp_w2_v[0], nsa_w_out[0], ln_mix_g[1], ln_mix_b[1])
    h = moe(h, 1)
    return h.reshape(B, T, D)
```
